```python
import jax, jax.numpy as jnp
from jax import lax
import numpy as np

D_MODEL = 1024
BATCH = 4
SEQ = 4096
DEPTH = 2
DEC_BATCH = 32
DEC_SEQ = 64
PAST_LEN = 1024

CHUNK = 64
N_META = 16
MIX_W = D_MODEL
CONV_W = MIX_W // 4
CONV_K = 3
MLA_HEADS = 8
QK_NOPE = 64
QK_ROPE = 32
V_DIM = 64
MLA_W = MLA_HEADS * V_DIM
Q_LORA = 256
KV_LORA = 128
RWKV_HEAD = 64
RWKV_W = MIX_W - CONV_W - MLA_W
RWKV_HEADS = RWKV_W // RWKV_HEAD
DECAY_LORA = 64
ICLR_LORA = 64
SHIFT_W = 3 * RWKV_W + DECAY_LORA + ICLR_LORA
IN_SPLITS = (CONV_W, CONV_W, CONV_W, CONV_W, Q_LORA, KV_LORA, QK_ROPE, MLA_W, SHIFT_W, RWKV_W)
IN_TOTAL = sum(IN_SPLITS)
IN_OFFSETS = tuple(np.cumsum(IN_SPLITS)[:-1].tolist())
RWKV_OFFSETS = (RWKV_W, 2 * RWKV_W, 3 * RWKV_W, 3 * RWKV_W + DECAY_LORA)
ROPE_BASE = 10000.0
RMS_EPS = 1e-6
GN_EPS = 64e-5
QBLOCK = 128
NEG = -1e30
FAR_CHUNK = 2 ** 30

kernel_name = 'hybrid_conv_mla_rwkv7_stream_step'


def rmsnorm(x, g):
    xf = x.astype(jnp.float32)
    y = xf * lax.rsqrt(jnp.mean(xf * xf, axis=-1, keepdims=True) + RMS_EPS)
    return (y * g.astype(jnp.float32)).astype(x.dtype)


def rope(x, pos):
    half = x.shape[-1] // 2
    inv = ROPE_BASE ** (-jnp.arange(half, dtype=jnp.float32) * 2.0 / x.shape[-1])
    ang = pos.astype(jnp.float32)[:, None] * inv[None, :]
    cos = jnp.cos(ang)[None, :, None, :]
    sin = jnp.sin(ang)[None, :, None, :]
    xf = x.astype(jnp.float32)
    x1, x2 = xf[..., :half], xf[..., half:]
    return jnp.concatenate([x1 * cos - x2 * sin, x1 * sin + x2 * cos], axis=-1).astype(x.dtype)


def chunk_attention(q_nope, q_rope, k_nope, k_rope, v, q_chunk, k_chunk):
    scale = (QK_NOPE + QK_ROPE) ** -0.5

    def block(args):
        qn, qr, qc = args
        s = (jnp.einsum('bqhd,bkhd->bhqk', qn, k_nope).astype(jnp.float32)
             + jnp.einsum('bqhr,bkr->bhqk', qr, k_rope).astype(jnp.float32)) * scale
        vis = k_chunk[None, :] <= qc[:, None]
        s = jnp.where(vis[None, None], s, NEG)
        p = jax.nn.softmax(s, axis=-1).astype(v.dtype)
        return jnp.einsum('bhqk,bkhd->bqhd', p, v)

    b, sq = q_nope.shape[:2]
    if sq <= QBLOCK:
        return block((q_nope, q_rope, q_chunk))
    nb = -(-sq // QBLOCK)
    pad = nb * QBLOCK - sq

    def blocks(t):
        t = jnp.pad(t, ((0, 0), (0, pad), (0, 0), (0, 0)))
        return jnp.moveaxis(t.reshape(b, nb, QBLOCK, *t.shape[2:]), 1, 0)

    qc = jnp.pad(q_chunk, (0, pad), constant_values=FAR_CHUNK).reshape(nb, QBLOCK)
    out = lax.map(block, (blocks(q_nope), blocks(q_rope), qc))
    out = jnp.moveaxis(out, 0, 1).reshape(b, nb * QBLOCK, *out.shape[3:])
    return out[:, :sq]


def rwkv7_scan(r, w, k, v, kk, a, s0):
    def step(s, inp):
        r_t, w_t, k_t, v_t, kk_t, a_t = inp
        sa = jnp.einsum('bhvk,bhk->bhv', s, -kk_t)
        s = (s * w_t[:, :, None, :] + sa[..., None] * (kk_t * a_t)[:, :, None, :]
             + v_t[..., None] * k_t[:, :, None, :])
        return s, jnp.einsum('bhvk,bhk->bhv', s, r_t)

    xs = tuple(jnp.moveaxis(t, 1, 0) for t in (r, w, k, v, kk, a))
    s, ys = lax.scan(step, s0, xs)
    return jnp.moveaxis(ys, 0, 1), s


def hybrid_layer(x, pos, q_chunk, past_chunk, ckv_past, krope_past, conv_st, shift_st, wkv_st,
                 norm_g, w_in, conv_w, q_norm_g, w_uq, kv_norm_g, w_ukv, shift_mu,
                 decay_w0, decay_w2, iclr_a0, iclr_a2, key_kk, key_ka, bonus_rk,
                 lnx_w, lnx_b, w_out):
    b, t = x.shape[:2]
    h = rmsnorm(x, norm_g)
    z = h @ w_in
    xin, bg, cg, ga, cq, ckv, kr, gb, zc, gc = jnp.split(z, IN_OFFSETS, axis=-1)

    u = cg * xin
    u_ext = jnp.concatenate([conv_st.astype(u.dtype), u], axis=1)
    conv = sum(conv_w[j] * u_ext[:, j:j + t] for j in range(CONV_K))
    y_a = bg * conv * jax.nn.silu(ga)
    new_conv = u_ext[:, -(CONV_K - 1):]

    q = (rmsnorm(cq, q_norm_g) @ w_uq).reshape(b, t, MLA_HEADS, QK_NOPE + QK_ROPE)
    q_nope = q[..., :QK_NOPE]
    q_rope = rope(q[..., QK_NOPE:], pos)
    ckv_n = rmsnorm(ckv, kv_norm_g)
    kr_r = rope(kr[:, :, None, :], pos)[:, :, 0]
    ckv_all = jnp.concatenate([ckv_past.astype(ckv_n.dtype), ckv_n], axis=1)
    kr_all = jnp.concatenate([krope_past.astype(kr_r.dtype), kr_r], axis=1)
    sk = ckv_all.shape[1]
    kv = (ckv_all @ w_ukv).reshape(b, sk, MLA_HEADS, QK_NOPE + V_DIM)
    k_chunk = jnp.concatenate([past_chunk, q_chunk])
    o = chunk_attention(q_nope, q_rope, kv[..., :QK_NOPE], kr_all, kv[..., QK_NOPE:],
                        q_chunk, k_chunk)
    y_b = o.reshape(b, t, MLA_W) * jax.nn.silu(gb)

    prev = jnp.concatenate([shift_st[:, None].astype(zc.dtype), zc[:, :-1]], axis=1)
    zs = zc + (prev - zc) * shift_mu
    r, k, v, wl, al = jnp.split(zs, RWKV_OFFSETS, axis=-1)
    w_log = -jax.nn.softplus(-(decay_w0 + jnp.tanh(wl) @ decay_w2).astype(jnp.float32)) - 0.5
    decay = jnp.exp(-jnp.exp(w_log))
    a = jax.nn.sigmoid((iclr_a0 + al @ iclr_a2).astype(jnp.float32))
    heads = lambda m: m.astype(jnp.float32).reshape(b, t, RWKV_HEADS, RWKV_HEAD)
    per_head = lambda p: p.astype(jnp.float32).reshape(RWKV_HEADS, RWKV_HEAD)
    r, k, v, decay, a = heads(r), heads(k), heads(v), heads(decay), heads(a)
    kk = k * per_head(key_kk)
    kk = kk / jnp.maximum(jnp.sqrt(jnp.sum(kk * kk, axis=-1, keepdims=True)), 1e-12)
    k = k * (1.0 + (a - 1.0) * per_head(key_ka))
    ys, s_new = rwkv7_scan(r, decay, k, v, kk, a, wkv_st.astype(jnp.float32))
    mu = jnp.mean(ys, axis=-1, keepdims=True)
    var = jnp.mean(jnp.square(ys - mu), axis=-1, keepdims=True)
    yn = (ys - mu) * lax.rsqrt(var + GN_EPS) * per_head(lnx_w) + per_head(lnx_b)
    yn = yn + jnp.sum(r * k * per_head(bonus_rk), axis=-1, keepdims=True) * v
    y_c = yn.reshape(b, t, RWKV_W).astype(x.dtype) * jax.nn.silu(gc)
    new_shift = zc[:, -1]

    y = jnp.concatenate([y_a, y_b, y_c], axis=-1) @ w_out
    return x + y, ckv_n, kr_r, new_conv, new_shift, s_new.astype(wkv_st.dtype)


def setup_inputs(seed: int = 0) -> dict:
    key = jax.random.key(seed)
    ks = jax.random.split(key, 32)
    f32 = jnp.float32
    nrm = lambda kk, shape, s: jax.random.normal(kk, shape, f32) * s
    return {
        'x_prompt': nrm(ks[0], (BATCH, SEQ, D_MODEL), 1.0),
        'x_sample': nrm(ks[1], (DEC_BATCH, DEC_SEQ, D_MODEL), 1.0),
        'cache_ckv': nrm(ks[2], (DEPTH, DEC_BATCH, PAST_LEN, KV_LORA), 1.0),
        'cache_krope': nrm(ks[3], (DEPTH, DEC_BATCH, PAST_LEN, QK_ROPE), 1.0),
        'state_conv': nrm(ks[4], (DEPTH, DEC_BATCH, CONV_K - 1, CONV_W), 1.0),
        'state_shift': nrm(ks[5], (DEPTH, DEC_BATCH, SHIFT_W), 1.0),
        'state_wkv': nrm(ks[6], (DEPTH, DEC_BATCH, RWKV_HEADS, RWKV_HEAD, RWKV_HEAD), 0.5),
        'meta_tokens': nrm(ks[7], (N_META, D_MODEL), 1.0),
        'norm_g': 1.0 + nrm(ks[8], (DEPTH, D_MODEL), 0.02),
        'w_in': nrm(ks[9], (DEPTH, D_MODEL, IN_TOTAL), D_MODEL ** -0.5),
        'conv_w': nrm(ks[10], (DEPTH, CONV_K, CONV_W), CONV_K ** -0.5),
        'q_norm_g': 1.0 + nrm(ks[11], (DEPTH, Q_LORA), 0.02),
        'w_uq': nrm(ks[12], (DEPTH, Q_LORA, MLA_HEADS * (QK_NOPE + QK_ROPE)), Q_LORA ** -0.5),
        'kv_norm_g': 1.0 + nrm(ks[13], (DEPTH, KV_LORA), 0.02),
        'w_ukv': nrm(ks[14], (DEPTH, KV_LORA, MLA_HEADS * (QK_NOPE + V_DIM)), KV_LORA ** -0.5),
        'shift_mu': jax.random.uniform(ks[15], (DEPTH, SHIFT_W), f32),
        'decay_w0': -1.0 + nrm(ks[16], (DEPTH, RWKV_W), 0.5),
        'decay_w2': nrm(ks[17], (DEPTH, DECAY_LORA, RWKV_W), 0.1 * DECAY_LORA ** -0.5),
        'iclr_a0': nrm(ks[18], (DEPTH, RWKV_W), 0.1),
        'iclr_a2': nrm(ks[19], (DEPTH, ICLR_LORA, RWKV_W), 0.1 * ICLR_LORA ** -0.5),
        'key_kk': 0.85 + nrm(ks[20], (DEPTH, RWKV_W), 0.05),
        'key_ka': 1.0 + nrm(ks[21], (DEPTH, RWKV_W), 0.05),
        'bonus_rk': nrm(ks[22], (DEPTH, RWKV_W), 0.1),
        'lnx_w': 1.0 + nrm(ks[23], (DEPTH, RWKV_W), 0.02),
        'lnx_b': nrm(ks[24], (DEPTH, RWKV_W), 0.02),
        'w_out': nrm(ks[25], (DEPTH, MIX_W, D_MODEL), (2.0 * DEPTH * MIX_W) ** -0.5),
        'final_g': 1.0 + nrm(ks[26], (D_MODEL,), 0.02),
    }


def stack_states(outs):
    return tuple(jnp.stack([o[i] for o in outs]) for i in range(5))


def reference(x_prompt, x_sample, cache_ckv, cache_krope, state_conv, state_shift, state_wkv,
              meta_tokens, norm_g, w_in, conv_w, q_norm_g, w_uq, kv_norm_g, w_ukv, shift_mu,
              decay_w0, decay_w2, iclr_a0, iclr_a2, key_kk, key_ka, bonus_rk, lnx_w, lnx_b,
              w_out, final_g):
    dt = x_prompt.dtype
    bp = x_prompt.shape[0]
    meta = jnp.broadcast_to(meta_tokens.astype(dt)[None], (bp, N_META, D_MODEL))
    hp = jnp.concatenate([meta, x_prompt], axis=1)
    tp = hp.shape[1]
    pos_p = jnp.arange(tp, dtype=jnp.int32)
    chunk_p = jnp.where(pos_p < N_META, -1, (pos_p - N_META) // CHUNK).astype(jnp.int32)
    past_chunk_p = jnp.zeros((0,), jnp.int32)
    ckv0 = jnp.zeros((bp, 0, KV_LORA), dt)
    kr0 = jnp.zeros((bp, 0, QK_ROPE), dt)
    conv0 = jnp.zeros((bp, CONV_K - 1, CONV_W), dt)
    shift0 = jnp.zeros((bp, SHIFT_W), dt)
    wkv0 = jnp.zeros((bp, RWKV_HEADS, RWKV_HEAD, RWKV_HEAD), dt)

    hs = x_sample
    ts = x_sample.shape[1]
    past = cache_ckv.shape[2]
    pos_s = past + jnp.arange(ts, dtype=jnp.int32)
    chunk_s = jnp.full((ts,), past // CHUNK, jnp.int32)
    past_chunk_s = jnp.arange(past, dtype=jnp.int32) // CHUNK

    outs_p, outs_s = [], []
    for l in range(DEPTH):
        lw = tuple(p[l] for p in (norm_g, w_in, conv_w, q_norm_g, w_uq, kv_norm_g, w_ukv,
                                   shift_mu, decay_w0, decay_w2, iclr_a0, iclr_a2, key_kk,
                                   key_ka, bonus_rk, lnx_w, lnx_b, w_out))
        hp, *st_p = hybrid_layer(hp, pos_p, chunk_p, past_chunk_p, ckv0, kr0, conv0, shift0,
                                 wkv0, *lw)
        hs, *st_s = hybrid_layer(hs, pos_s, chunk_s, past_chunk_s, cache_ckv[l], cache_krope[l],
                                 state_conv[l], state_shift[l], state_wkv[l], *lw)
        outs_p.append(st_p)
        outs_s.append(st_s)

    y_prompt = rmsnorm(hp[:, N_META:], final_g)
    y_sample = rmsnorm(hs, final_g)
    ckv_p, kr_p, conv_p, shift_p, wkv_p = stack_states(outs_p)
    ckv_s, kr_s, conv_s, shift_s, wkv_s = stack_states(outs_s)
    return (y_prompt, y_sample, ckv_p, kr_p, conv_p, shift_p, wkv_p,
            ckv_s, kr_s, conv_s, shift_s, wkv_s)
```

```python
import functools

import jax
import jax.numpy as jnp
import numpy as np
from jax import lax
from jax.experimental import pallas as pl
from jax.experimental.pallas import tpu as pltpu

D_MODEL = 1024
CHUNK = 64
N_META = 16
CONV_W = 256
CONV_K = 3
MLA_HEADS = 8
QK_NOPE = 64
QK_ROPE = 32
V_DIM = 64
MLA_W = MLA_HEADS * V_DIM
Q_LORA = 256
KV_LORA = 128
RWKV_HEAD = 64
RWKV_W = 256
RWKV_HEADS = 4
DECAY_LORA = 64
ICLR_LORA = 64
SHIFT_W = 3 * RWKV_W + DECAY_LORA + ICLR_LORA
ROPE_BASE = 10000.0
RMS_EPS = 1e-6
GN_EPS = 64e-5
NEG = -1e30

LANES = 128
QTILE = 128
FRONT = QTILE - N_META
QK_W = 2 * LANES
SCAN_GROUP = 4
VMEM_LIMIT = 48 * 1024 * 1024

F32 = jnp.float32
BF16 = jnp.bfloat16


def _dot(a, b):
    return jnp.dot(a, b, preferred_element_type=F32)


def _dot_nt(a, b):
    return lax.dot_general(a, b, (((1,), (1,)), ((), ())), preferred_element_type=F32)


def _rms(x, g):
    return x * lax.rsqrt(jnp.mean(x * x, axis=-1, keepdims=True) + RMS_EPS) * g


def _silu(x):
    return x * jax.nn.sigmoid(x)


def _rope_block(x, cos, sin_lo, sin_hi):
    return (x * cos + pltpu.roll(x, LANES - QK_ROPE // 2, 1) * sin_lo
            + pltpu.roll(x, QK_ROPE // 2, 1) * sin_hi)


def _proj_kernel(x_ref, g_ref, wa_ref, wq_ref, wckv_ref, wkr_ref, wgb_ref, wzc_ref, wgc_ref,
                 qg_ref, wuq_ref, bd_ref, kvg_ref, cos_ref, slo_ref, shi_ref,
                 za_ref, qp_ref, kcat_ref, ckv_ref, kr_ref, gb_ref, zc_ref, gc_ref):
    scale = (QK_NOPE + QK_ROPE) ** -0.5
    hb = _rms(x_ref[...], g_ref[...]).astype(BF16)
    za_ref[...] = _dot(hb, wa_ref[...])
    gb_ref[...] = _dot(hb, wgb_ref[...])
    zc_ref[...] = _dot(hb, wzc_ref[...])
    gc_ref[...] = _dot(hb, wgc_ref[...])

    cos, slo, shi = cos_ref[...], slo_ref[...], shi_ref[...]
    cqn = _rms(_dot(hb, wq_ref[...]), qg_ref[...]).astype(BF16)
    qe = _dot(cqn, wuq_ref[...])
    nope_w = MLA_HEADS * QK_NOPE
    qabs = _dot(qe[:, :nope_w].astype(BF16), bd_ref[...])
    for h in range(MLA_HEADS):
        lo, hi = h * LANES, (h + 1) * LANES
        rot = _rope_block(qe[:, nope_w + lo:nope_w + hi], cos, slo, shi)
        qp_ref[h, :, 0:LANES] = (qabs[:, lo:hi] * scale).astype(BF16)
        qp_ref[h, :, LANES:QK_W] = (rot * scale).astype(BF16)

    ckvn = _rms(_dot(hb, wckv_ref[...]), kvg_ref[...])
    ckv_ref[...] = ckvn
    krr = _rope_block(_dot(hb, wkr_ref[...]), cos, slo, shi)
    kr_ref[...] = krr[:, :QK_ROPE]
    kcat_ref[:, 0:LANES] = ckvn.astype(BF16)
    kcat_ref[:, LANES:QK_W] = krr.astype(BF16)


def _proj(x, lw, tabs, tm, tab_blocks):
    rows = x.shape[0]
    n = rows // tm
    row = lambda w: pl.BlockSpec((tm, w), lambda i: (i, 0))
    full = lambda a: pl.BlockSpec(a.shape, lambda i: (0,) * a.ndim)
    tab = pl.BlockSpec((tm, LANES), lambda i: (i % tab_blocks, 0))
    weights = (lw["norm_g"], lw["wa"], lw["wq"], lw["wckv"], lw["wkr"], lw["wgb"], lw["wzc"],
               lw["wgc"], lw["q_norm_g"], lw["wuq"], lw["bd"], lw["kv_norm_g"])
    out_shape = (
        jax.ShapeDtypeStruct((rows, 4 * CONV_W), F32),
        jax.ShapeDtypeStruct((MLA_HEADS, rows, QK_W), BF16),
        jax.ShapeDtypeStruct((rows, QK_W), BF16),
        jax.ShapeDtypeStruct((rows, KV_LORA), F32),
        jax.ShapeDtypeStruct((rows, QK_ROPE), F32),
        jax.ShapeDtypeStruct((rows, MLA_W), F32),
        jax.ShapeDtypeStruct((rows, SHIFT_W), F32),
        jax.ShapeDtypeStruct((rows, RWKV_W), F32),
    )
    out_specs = (row(4 * CONV_W), pl.BlockSpec((MLA_HEADS, tm, QK_W), lambda i: (0, i, 0)),
                 row(QK_W), row(KV_LORA), row(QK_ROPE), row(MLA_W), row(SHIFT_W), row(RWKV_W))
    return pl.pallas_call(
        _proj_kernel,
        grid=(n,),
        in_specs=[row(D_MODEL)] + [full(w) for w in weights] + [tab, tab, tab],
        out_specs=out_specs,
        out_shape=out_shape,
        compiler_params=pltpu.CompilerParams(dimension_semantics=("parallel",),
                                             vmem_limit_bytes=VMEM_LIMIT),
        name="proj",
    )(x, *weights, *tabs)


def _heads_out(o_lat, wv_ref, rows):
    out = None
    for h in range(MLA_HEADS):
        part = _dot(o_lat[h * rows:(h + 1) * rows].astype(BF16), wv_ref[h])
        out = part if out is None else out + part
    return out


def _attn_prompt_kernel(q_ref, k_ref, gb_ref, wv_ref, yb_ref, m_ref, l_ref, acc_ref):
    i = pl.program_id(1)
    rows = MLA_HEADS * QTILE
    q = q_ref[...].reshape(rows, QK_W)
    m_ref[...] = jnp.full((rows, 1), NEG, F32)
    l_ref[...] = jnp.zeros((rows, 1), F32)
    acc_ref[...] = jnp.zeros((rows, KV_LORA), F32)

    def tile(j, mask):
        kt = k_ref[0, pl.ds(pl.multiple_of(j * QTILE, QTILE), QTILE), :]
        s = _dot_nt(q, kt)
        if mask is not None:
            s = jnp.where(mask, s, NEG)
        m_prev = m_ref[...]
        m_new = jnp.maximum(m_prev, jnp.max(s, axis=-1, keepdims=True))
        alpha = jnp.exp(m_prev - m_new)
        p = jnp.exp(s - m_new)
        l_ref[...] = alpha * l_ref[...] + jnp.sum(p, axis=-1, keepdims=True)
        acc_ref[...] = alpha * acc_ref[...] + _dot(p.astype(BF16), kt[:, :KV_LORA])
        m_ref[...] = m_new

    col = lax.broadcasted_iota(jnp.int32, (rows, QTILE), 1)
    row = lax.broadcasted_iota(jnp.int32, (rows, QTILE), 0) & (QTILE - 1)
    tile(0, col >= FRONT)

    def mid(j, carry):
        tile(j, None)
        return carry

    lax.fori_loop(1, i, mid, 0)

    @pl.when(i > 0)
    def _():
        tile(i, (col < CHUNK) | (row >= CHUNK))

    o_lat = acc_ref[...] / l_ref[...]
    yb_ref[...] = (_heads_out(o_lat, wv_ref, QTILE) * _silu(gb_ref[...])).astype(BF16)


def _attn_prompt(qp, kcat, gb, wv, batch, tp):
    nq = tp // QTILE
    return pl.pallas_call(
        _attn_prompt_kernel,
        grid=(batch, nq),
        in_specs=[
            pl.BlockSpec((MLA_HEADS, QTILE, QK_W), lambda b, i: (0, b * nq + i, 0)),
            pl.BlockSpec((1, tp, QK_W), lambda b, i: (b, 0, 0)),
            pl.BlockSpec((QTILE, MLA_W), lambda b, i: (b * nq + i, 0)),
            pl.BlockSpec(wv.shape, lambda b, i: (0, 0, 0)),
        ],
        out_specs=pl.BlockSpec((QTILE, MLA_W), lambda b, i: (b * nq + i, 0)),
        out_shape=jax.ShapeDtypeStruct((batch * tp, MLA_W), BF16),
        scratch_shapes=[pltpu.VMEM((MLA_HEADS * QTILE, 1), F32),
                        pltpu.VMEM((MLA_HEADS * QTILE, 1), F32),
                        pltpu.VMEM((MLA_HEADS * QTILE, KV_LORA), F32)],
        compiler_params=pltpu.CompilerParams(dimension_semantics=("parallel", "arbitrary"),
                                             vmem_limit_bytes=VMEM_LIMIT),
        name="attn_prompt",
    )(qp, kcat.reshape(batch, tp, QK_W), gb, wv)


def _attn_sample_kernel(q_ref, kn_ref, ck_ref, kr_ref, gb_ref, wv_ref, yb_ref):
    ts = q_ref.shape[1]
    q = q_ref[...].reshape(MLA_HEADS * ts, QK_W)
    kn = kn_ref[...]
    ck = ck_ref[0, 0].astype(BF16)
    kr = kr_ref[0, 0].astype(BF16)
    s_past = _dot_nt(q[:, :KV_LORA], ck) + _dot_nt(q[:, KV_LORA:KV_LORA + QK_ROPE], kr)
    s_new = _dot_nt(q, kn)
    m = jnp.maximum(jnp.max(s_past, axis=-1, keepdims=True),
                    jnp.max(s_new, axis=-1, keepdims=True))
    p_past = jnp.exp(s_past - m)
    p_new = jnp.exp(s_new - m)
    l = jnp.sum(p_past, axis=-1, keepdims=True) + jnp.sum(p_new, axis=-1, keepdims=True)
    o_lat = (_dot(p_past.astype(BF16), ck) + _dot(p_new.astype(BF16), kn[:, :KV_LORA])) / l
    yb_ref[...] = (_heads_out(o_lat, wv_ref, ts) * _silu(gb_ref[...])).astype(BF16)


def _attn_sample(qp, kcat, cache_ckv, cache_krope, layer, gb, wv, batch, ts):
    past = cache_ckv.shape[2]
    return pl.pallas_call(
        _attn_sample_kernel,
        grid=(batch,),
        in_specs=[
            pl.BlockSpec((MLA_HEADS, ts, QK_W), lambda b: (0, b, 0)),
            pl.BlockSpec((ts, QK_W), lambda b: (b, 0)),
            pl.BlockSpec((1, 1, past, KV_LORA), lambda b: (layer, b, 0, 0)),
            pl.BlockSpec((1, 1, past, QK_ROPE), lambda b: (layer, b, 0, 0)),
            pl.BlockSpec((ts, MLA_W), lambda b: (b, 0)),
            pl.BlockSpec(wv.shape, lambda b: (0, 0, 0)),
        ],
        out_specs=pl.BlockSpec((ts, MLA_W), lambda b: (b, 0)),
        out_shape=jax.ShapeDtypeStruct((batch * ts, MLA_W), BF16),
        compiler_params=pltpu.CompilerParams(dimension_semantics=("parallel",),
                                             vmem_limit_bytes=VMEM_LIMIT),
        name="attn_sample",
    )(qp, kcat, cache_ckv, cache_krope, gb, wv)


def _head_sums(x, head_of_lane):
    out = jnp.zeros_like(x)
    for h in range(RWKV_HEADS):
        sel = head_of_lane == h
        s = jnp.sum(jnp.where(sel, x, 0.0), axis=-1, keepdims=True)
        out = jnp.where(sel, s, out)
    return out


def _mix_kernel(za_ref, zc_ref, gc_ref, conv0_ref, shift0_ref, wkv0_ref,
                cw_ref, mu_ref, w0_ref, w2a2_ref, a0_ref, kkw_ref, kaw_ref, bonus_ref,
                lnw_ref, lnb_ref,
                ya_ref, yc_ref, conv_ref, shift_ref, wkv_ref,
                s_ref, cu_ref, cz_ref, vec_ref, vt_ref, yt_ref):
    tb = pl.program_id(1)
    nblk = pl.num_programs(1)
    tblk = za_ref.shape[1]
    nh = SCAN_GROUP * RWKV_HEADS

    @pl.when(tb == 0)
    def _():
        s_ref[...] = wkv0_ref[...].reshape(nh, RWKV_HEAD, RWKV_HEAD)
        cu_ref[...] = conv0_ref[...]
        cz_ref[...] = shift0_ref[...]

    trow = lax.broadcasted_iota(jnp.int32, (tblk, 1), 0)
    head_of_lane = lax.broadcasted_iota(jnp.int32, (1, RWKV_W), 1) // RWKV_HEAD
    lane128 = lax.broadcasted_iota(jnp.int32, (1, LANES), 1)
    cw = cw_ref[...]
    bonus_v = []
    for b in range(SCAN_GROUP):
        za = za_ref[b]
        xin, bg = za[:, 0:CONV_W], za[:, CONV_W:2 * CONV_W]
        cg, ga = za[:, 2 * CONV_W:3 * CONV_W], za[:, 3 * CONV_W:4 * CONV_W]
        u = cg * xin
        cu = cu_ref[b]
        u1 = jnp.where(trow == 0, cu[1:2], pltpu.roll(u, 1, 0))
        u2 = jnp.where(trow == 0, cu[0:1], jnp.where(trow == 1, cu[1:2], pltpu.roll(u, 2, 0)))
        conv = cw[0:1] * u2 + cw[1:2] * u1 + cw[2:3] * u
        ya_ref[b] = (bg * conv * _silu(ga)).astype(BF16)
        cu_ref[b] = u[tblk - (CONV_K - 1):tblk]

        zc = zc_ref[b]
        prev = jnp.where(trow == 0, cz_ref[b], pltpu.roll(zc, 1, 0))
        cz_ref[b] = zc[tblk - 1:tblk]
        zs = zc + (prev - zc) * mu_ref[...]
        r, k, v = zs[:, 0:RWKV_W], zs[:, RWKV_W:2 * RWKV_W], zs[:, 2 * RWKV_W:3 * RWKV_W]
        la = zs[:, 3 * RWKV_W:SHIFT_W]
        la = jnp.where(lane128 < DECAY_LORA, jnp.tanh(la), la)
        lora = _dot(la.astype(BF16), w2a2_ref[...])
        dec_in = w0_ref[...] + lora[:, 0:RWKV_W]
        w_log = -(jnp.maximum(-dec_in, 0.0) + jnp.log1p(jnp.exp(-jnp.abs(dec_in)))) - 0.5
        decay = jnp.exp(-jnp.exp(w_log))
        a = jax.nn.sigmoid(a0_ref[...] + lora[:, RWKV_W:2 * RWKV_W])
        kk = k * kkw_ref[...]
        kk = kk / jnp.maximum(jnp.sqrt(_head_sums(kk * kk, head_of_lane)), 1e-12)
        kp = k * (1.0 + (a - 1.0) * kaw_ref[...])
        bonus_v.append(_head_sums(r * kp * bonus_ref[...], head_of_lane) * v)

        vecs = (-kk, decay, kk * a, kp, r)
        for h in range(RWKV_HEADS):
            for n, vec in enumerate(vecs):
                vec_ref[n, b * RWKV_HEADS + h, 0:tblk, :] = vec[:, h * RWKV_HEAD:(h + 1) * RWKV_HEAD]
        if tblk < LANES:
            v = jnp.concatenate([v, jnp.zeros((LANES - tblk, RWKV_W), F32)], axis=0)
        vt = v.T
        for h in range(RWKV_HEADS):
            vt_ref[b * RWKV_HEADS + h] = vt[h * RWKV_HEAD:(h + 1) * RWKV_HEAD]

    def step(t, carry):
        onehot = lane128 == t
        for bh in range(nh):
            s = s_ref[bh]
            nkk = vec_ref[0, bh, pl.ds(t, 1), :]
            w = vec_ref[1, bh, pl.ds(t, 1), :]
            kka = vec_ref[2, bh, pl.ds(t, 1), :]
            kp = vec_ref[3, bh, pl.ds(t, 1), :]
            r = vec_ref[4, bh, pl.ds(t, 1), :]
            vcol = jnp.sum(jnp.where(onehot, vt_ref[bh], 0.0), axis=1, keepdims=True)
            sa = jnp.sum(s * nkk, axis=1, keepdims=True)
            s = s * w + sa * kka + vcol * kp
            s_ref[bh] = s
            y = jnp.sum(s * r, axis=1, keepdims=True)
            yt_ref[bh] = jnp.where(onehot, y, yt_ref[bh])
        return carry

    yt_ref[...] = jnp.zeros(yt_ref.shape, F32)
    lax.fori_loop(0, tblk, step, 0)

    for b in range(SCAN_GROUP):
        yt = jnp.concatenate([yt_ref[b * RWKV_HEADS + h] for h in range(RWKV_HEADS)], axis=0)
        ys = yt.T[0:tblk]
        mean = _head_sums(ys, head_of_lane) * (1.0 / RWKV_HEAD)
        d = ys - mean
        var = _head_sums(d * d, head_of_lane) * (1.0 / RWKV_HEAD)
        yn = d * lax.rsqrt(var + GN_EPS) * lnw_ref[...] + lnb_ref[...]
        yn = yn + bonus_v[b]
        yc_ref[b] = (yn * _silu(gc_ref[b])).astype(BF16)

    @pl.when(tb == nblk - 1)
    def _():
        conv_ref[...] = cu_ref[...]
        shift_ref[...] = cz_ref[...]
        wkv_ref[...] = s_ref[...].reshape(SCAN_GROUP, RWKV_HEADS, RWKV_HEAD, RWKV_HEAD)


def _mix(za, zc, gc, conv0, shift0, wkv0, lw, batch, t, tblk):
    groups = batch // SCAN_GROUP
    nblk = t // tblk
    nh = SCAN_GROUP * RWKV_HEADS
    seq = lambda w: pl.BlockSpec((SCAN_GROUP, tblk, w), lambda g, i: (g, i, 0))
    state3 = lambda a: pl.BlockSpec((SCAN_GROUP,) + a.shape[1:], lambda g, i: (g, 0, 0))
    state4 = pl.BlockSpec((SCAN_GROUP, RWKV_HEADS, RWKV_HEAD, RWKV_HEAD),
                          lambda g, i: (g, 0, 0, 0))
    full = lambda a: pl.BlockSpec(a.shape, lambda g, i: (0,) * a.ndim)
    params = (lw["conv_w"], lw["shift_mu"], lw["decay_w0"], lw["w2a2"], lw["iclr_a0"],
              lw["key_kk"], lw["key_ka"], lw["bonus_rk"], lw["lnx_w"], lw["lnx_b"])
    out_shape = (
        jax.ShapeDtypeStruct((batch, t, CONV_W), BF16),
        jax.ShapeDtypeStruct((batch, t, RWKV_W), BF16),
        jax.ShapeDtypeStruct(conv0.shape, F32),
        jax.ShapeDtypeStruct(shift0.shape, F32),
        jax.ShapeDtypeStruct(wkv0.shape, F32),
    )
    return pl.pallas_call(
        _mix_kernel,
        grid=(groups, nblk),
        in_specs=[seq(4 * CONV_W), seq(SHIFT_W), seq(RWKV_W), state3(conv0), state3(shift0),
                  state4] + [full(p) for p in params],
        out_specs=(seq(CONV_W), seq(RWKV_W), state3(conv0), state3(shift0), state4),
        out_shape=out_shape,
        scratch_shapes=[
            pltpu.VMEM((nh, RWKV_HEAD, RWKV_HEAD), F32),
            pltpu.VMEM((SCAN_GROUP, CONV_K - 1, CONV_W), F32),
            pltpu.VMEM((SCAN_GROUP, 1, SHIFT_W), F32),
            pltpu.VMEM((5, nh, LANES, RWKV_HEAD), F32),
            pltpu.VMEM((nh, RWKV_HEAD, LANES), F32),
            pltpu.VMEM((nh, RWKV_HEAD, LANES), F32),
        ],
        compiler_params=pltpu.CompilerParams(dimension_semantics=("parallel", "arbitrary"),
                                             vmem_limit_bytes=VMEM_LIMIT),
        name="mix",
    )(za, zc, gc, conv0, shift0, wkv0, *params)


def _out_kernel(ya_ref, yb_ref, yc_ref, x_ref, wa_ref, wb_ref, wc_ref, fg_ref, o_ref, *, final):
    y = (_dot(ya_ref[...], wa_ref[...]) + _dot(yb_ref[...], wb_ref[...])
         + _dot(yc_ref[...], wc_ref[...]))
    y = x_ref[...] + y
    if final:
        y = _rms(y, fg_ref[...])
    o_ref[...] = y


def _out(ya, yb, yc, x, lw, final_g, final, tm):
    rows = x.shape[0]
    row = lambda w: pl.BlockSpec((tm, w), lambda i: (i, 0))
    full = lambda a: pl.BlockSpec(a.shape, lambda i: (0,) * a.ndim)
    weights = (lw["wo_a"], lw["wo_b"], lw["wo_c"], final_g)
    return pl.pallas_call(
        functools.partial(_out_kernel, final=final),
        grid=(rows // tm,),
        in_specs=[row(CONV_W), row(MLA_W), row(RWKV_W), row(D_MODEL)] + [full(w) for w in weights],
        out_specs=row(D_MODEL),
        out_shape=jax.ShapeDtypeStruct((rows, D_MODEL), F32),
        compiler_params=pltpu.CompilerParams(dimension_semantics=("parallel",),
                                             vmem_limit_bytes=VMEM_LIMIT),
        name="out",
    )(ya, yb, yc, x, *weights)


def _layer_weights(l, norm_g, w_in, conv_w, q_norm_g, w_uq, kv_norm_g, w_ukv, shift_mu,
                   decay_w0, decay_w2, iclr_a0, iclr_a2, key_kk, key_ka, bonus_rk, lnx_w, lnx_b,
                   w_out):
    row = lambda p: p[l].reshape(1, -1).astype(F32)
    wi = w_in[l].astype(BF16)
    o = np.cumsum([0, 4 * CONV_W, Q_LORA, KV_LORA, QK_ROPE, MLA_W, SHIFT_W, RWKV_W]).tolist()
    cols = lambda j: wi[:, o[j]:o[j + 1]]
    uq = w_uq[l].astype(BF16).reshape(Q_LORA, MLA_HEADS, QK_NOPE + QK_ROPE)
    uq_rope = jnp.pad(uq[:, :, QK_NOPE:], ((0, 0), (0, 0), (0, LANES - QK_ROPE)))
    ukv = w_ukv[l].astype(BF16).reshape(KV_LORA, MLA_HEADS, QK_NOPE + V_DIM)
    eye = jnp.eye(MLA_HEADS, dtype=BF16)
    bd = jnp.einsum("chd,hg->hdgc", ukv[:, :, :QK_NOPE], eye).reshape(
        MLA_HEADS * QK_NOPE, MLA_HEADS * KV_LORA)
    wv = jnp.einsum("che,hg->hcge", ukv[:, :, QK_NOPE:], eye).reshape(
        MLA_HEADS, KV_LORA, MLA_W)
    zeros = jnp.zeros((DECAY_LORA, RWKV_W), BF16)
    w2a2 = jnp.concatenate([
        jnp.concatenate([decay_w2[l].astype(BF16), zeros], axis=1),
        jnp.concatenate([zeros, iclr_a2[l].astype(BF16)], axis=1)], axis=0)
    wo = w_out[l].astype(BF16)
    return {
        "norm_g": row(norm_g), "wa": cols(0), "wq": cols(1), "wckv": cols(2),
        "wkr": jnp.pad(cols(3), ((0, 0), (0, LANES - QK_ROPE))),
        "wgb": cols(4), "wzc": cols(5), "wgc": cols(6),
        "q_norm_g": row(q_norm_g),
        "wuq": jnp.concatenate([uq[:, :, :QK_NOPE].reshape(Q_LORA, -1),
                                uq_rope.reshape(Q_LORA, -1)], axis=1),
        "bd": bd, "wv": wv, "kv_norm_g": row(kv_norm_g),
        "conv_w": conv_w[l].astype(F32), "shift_mu": row(shift_mu), "decay_w0": row(decay_w0),
        "w2a2": w2a2, "iclr_a0": row(iclr_a0), "key_kk": row(key_kk), "key_ka": row(key_ka),
        "bonus_rk": row(bonus_rk), "lnx_w": row(lnx_w), "lnx_b": row(lnx_b),
        "wo_a": wo[0:CONV_W], "wo_b": wo[CONV_W:CONV_W + MLA_W], "wo_c": wo[CONV_W + MLA_W:],
    }


def _rope_tables(pos):
    half = QK_ROPE // 2
    inv = ROPE_BASE ** (-jnp.arange(half, dtype=F32) * 2.0 / QK_ROPE)
    ang = pos.astype(F32)[:, None] * inv[None, :]
    cos, sin = jnp.cos(ang), jnp.sin(ang)
    z = jnp.zeros((pos.shape[0], LANES - QK_ROPE), F32)
    zh = jnp.zeros((pos.shape[0], half), F32)
    return (jnp.concatenate([cos, cos, z], axis=1),
            jnp.concatenate([-sin, zh, z], axis=1),
            jnp.concatenate([zh, sin, z], axis=1))


def _stream(x, batch, t, tm, tabs, tab_blocks, tblk, states, attn, lws, final_g):
    outs = []
    depth = len(lws)
    for l, lw in enumerate(lws):
        za, qp, kcat, ckv, kr, gb, zc, gc = _proj(x, lw, tabs, tm, tab_blocks)
        yb = attn(l, lw, qp, kcat, gb)
        conv0, shift0, wkv0 = states[l]
        ya, yc, conv, shift, wkv = _mix(
            za.reshape(batch, t, -1), zc.reshape(batch, t, -1), gc.reshape(batch, t, -1),
            conv0, shift0, wkv0, lw, batch, t, tblk)
        x = _out(ya.reshape(batch * t, -1), yb, yc.reshape(batch * t, -1), x, lw, final_g,
                 l == depth - 1, tm)
        outs.append((ckv.reshape(batch, t, -1), kr.reshape(batch, t, -1), conv,
                     shift.reshape(batch, -1), wkv))
    return x, outs


def kernel(x_prompt, x_sample, cache_ckv, cache_krope, state_conv, state_shift, state_wkv,
           meta_tokens, norm_g, w_in, conv_w, q_norm_g, w_uq, kv_norm_g, w_ukv, shift_mu,
           decay_w0, decay_w2, iclr_a0, iclr_a2, key_kk, key_ka, bonus_rk, lnx_w, lnx_b,
           w_out, final_g):
    depth = w_in.shape[0]
    bp, seq = x_prompt.shape[:2]
    bs, ts = x_sample.shape[:2]
    past = cache_ckv.shape[2]
    tp = FRONT + N_META + seq
    assert tp % (3 * QTILE) == 0 and bp % SCAN_GROUP == 0 and bs % SCAN_GROUP == 0
    lws = [_layer_weights(l, norm_g, w_in, conv_w, q_norm_g, w_uq, kv_norm_g, w_ukv, shift_mu,
                          decay_w0, decay_w2, iclr_a0, iclr_a2, key_kk, key_ka, bonus_rk,
                          lnx_w, lnx_b, w_out) for l in range(depth)]
    fg = final_g.reshape(1, -1).astype(F32)

    xp = jnp.concatenate([
        jnp.zeros((bp, FRONT, D_MODEL), F32),
        jnp.broadcast_to(meta_tokens.astype(F32)[None], (bp, N_META, D_MODEL)),
        x_prompt], axis=1).reshape(bp * tp, D_MODEL)
    tm_p = 3 * QTILE
    tabs_p = _rope_tables(jnp.maximum(jnp.arange(tp, dtype=jnp.int32) - FRONT, 0))
    zero_states = [(jnp.zeros((bp, CONV_K - 1, CONV_W), F32), jnp.zeros((bp, 1, SHIFT_W), F32),
                    jnp.zeros((bp, RWKV_HEADS, RWKV_HEAD, RWKV_HEAD), F32))] * depth
    attn_p = lambda l, lw, qp, kcat, gb: _attn_prompt(qp, kcat, gb, lw["wv"], bp, tp)
    yp, outs_p = _stream(xp, bp, tp, tm_p, tabs_p, tp // tm_p, QTILE, zero_states, attn_p, lws, fg)

    tm_s = 512
    reps = tm_s // ts
    tabs_s = tuple(jnp.tile(tb_, (reps, 1))
                   for tb_ in _rope_tables(past + jnp.arange(ts, dtype=jnp.int32)))
    states_s = [(state_conv[l], state_shift[l].reshape(bs, 1, SHIFT_W), state_wkv[l])
                for l in range(depth)]
    attn_s = lambda l, lw, qp, kcat, gb: _attn_sample(qp, kcat, cache_ckv, cache_krope, l, gb,
                                                      lw["wv"], bs, ts)
    ys, outs_s = _stream(x_sample.reshape(bs * ts, D_MODEL), bs, ts, tm_s, tabs_s, 1, ts,
                         states_s, attn_s, lws, fg)

    stack = lambda outs, j: jnp.stack([o[j] for o in outs])
    y_prompt = yp.reshape(bp, tp, D_MODEL)[:, FRONT + N_META:]
    ckv_p = stack(outs_p, 0)[:, :, FRONT:]
    kr_p = stack(outs_p, 1)[:, :, FRONT:]
    return (y_prompt, ys.reshape(bs, ts, D_MODEL), ckv_p, kr_p, stack(outs_p, 2),
            stack(outs_p, 3), stack(outs_p, 4),
            stack(outs_s, 0), stack(outs_s, 1), stack(outs_s, 2), stack(outs_s, 3),
            stack(outs_s, 4))
```

```python
import functools

import jax
import jax.numpy as jnp
import numpy as np
from jax import lax
from jax.experimental import pallas as pl
from jax.experimental.pallas import tpu as pltpu

D_MODEL = 1024
CHUNK = 64
N_META = 16
CONV_W = 256
CONV_K = 3
MLA_HEADS = 8
QK_NOPE = 64
QK_ROPE = 32
V_DIM = 64
MLA_W = MLA_HEADS * V_DIM
Q_LORA = 256
KV_LORA = 128
RWKV_HEAD = 64
RWKV_W = 256
RWKV_HEADS = 4
DECAY_LORA = 64
ICLR_LORA = 64
SHIFT_W = 3 * RWKV_W + DECAY_LORA + ICLR_LORA
ROPE_BASE = 10000.0
RMS_EPS = 1e-6
GN_EPS = 64e-5
NEG = -1e30

LANES = 128
QTILE = 128
FRONT = QTILE - N_META
QK_W = 2 * LANES
SCAN_GROUP = 4
SCAN_T = 64
VMEM_LIMIT = 48 * 1024 * 1024

F32 = jnp.float32
BF16 = jnp.bfloat16


def _dot(a, b):
    return jnp.dot(a, b, preferred_element_type=F32)


def _dot_nt(a, b):
    return lax.dot_general(a, b, (((1,), (1,)), ((), ())), preferred_element_type=F32)


def _rms(x, g):
    return x * lax.rsqrt(jnp.mean(x * x, axis=-1, keepdims=True) + RMS_EPS) * g


def _silu(x):
    return x * jax.nn.sigmoid(x)


def _rope_block(x, cos, sin_lo, sin_hi):
    return (x * cos + pltpu.roll(x, LANES - QK_ROPE // 2, 1) * sin_lo
            + pltpu.roll(x, QK_ROPE // 2, 1) * sin_hi)


def _proj_kernel(x_ref, g_ref, wa_ref, wq_ref, wckv_ref, wkr_ref, wgb_ref, wzc_ref, wgc_ref,
                 qg_ref, wuq_ref, bd_ref, kvg_ref, cos_ref, slo_ref, shi_ref,
                 za_ref, qp_ref, kcat_ref, ckv_ref, kr_ref, gb_ref, zc_ref, gc_ref):
    scale = (QK_NOPE + QK_ROPE) ** -0.5
    hb = _rms(x_ref[...], g_ref[...]).astype(BF16)
    za_ref[...] = _dot(hb, wa_ref[...])
    gb_ref[...] = _dot(hb, wgb_ref[...])
    zc_ref[...] = _dot(hb, wzc_ref[...])
    gc_ref[...] = _dot(hb, wgc_ref[...])

    cos, slo, shi = cos_ref[...], slo_ref[...], shi_ref[...]
    cqn = _rms(_dot(hb, wq_ref[...]), qg_ref[...]).astype(BF16)
    qe = _dot(cqn, wuq_ref[...])
    nope_w = MLA_HEADS * QK_NOPE
    qabs = _dot(qe[:, :nope_w].astype(BF16), bd_ref[...])
    for h in range(MLA_HEADS):
        lo, hi = h * LANES, (h + 1) * LANES
        rot = _rope_block(qe[:, nope_w + lo:nope_w + hi], cos, slo, shi)
        qp_ref[h, :, 0:LANES] = (qabs[:, lo:hi] * scale).astype(BF16)
        qp_ref[h, :, LANES:QK_W] = (rot * scale).astype(BF16)

    ckvn = _rms(_dot(hb, wckv_ref[...]), kvg_ref[...])
    ckv_ref[...] = ckvn
    krr = _rope_block(_dot(hb, wkr_ref[...]), cos, slo, shi)
    kr_ref[...] = krr[:, :QK_ROPE]
    kcat_ref[:, 0:LANES] = ckvn.astype(BF16)
    kcat_ref[:, LANES:QK_W] = krr.astype(BF16)


def _proj(x, lw, tabs, tm, tab_blocks):
    rows = x.shape[0]
    n = rows // tm
    row = lambda w: pl.BlockSpec((tm, w), lambda i: (i, 0))
    full = lambda a: pl.BlockSpec(a.shape, lambda i: (0,) * a.ndim)
    tab = pl.BlockSpec((tm, LANES), lambda i: (i % tab_blocks, 0))
    weights = (lw["norm_g"], lw["wa"], lw["wq"], lw["wckv"], lw["wkr"], lw["wgb"], lw["wzc"],
               lw["wgc"], lw["q_norm_g"], lw["wuq"], lw["bd"], lw["kv_norm_g"])
    out_shape = (
        jax.ShapeDtypeStruct((rows, 4 * CONV_W), F32),
        jax.ShapeDtypeStruct((MLA_HEADS, rows, QK_W), BF16),
        jax.ShapeDtypeStruct((rows, QK_W), BF16),
        jax.ShapeDtypeStruct((rows, KV_LORA), F32),
        jax.ShapeDtypeStruct((rows, QK_ROPE), F32),
        jax.ShapeDtypeStruct((rows, MLA_W), F32),
        jax.ShapeDtypeStruct((rows, SHIFT_W), F32),
        jax.ShapeDtypeStruct((rows, RWKV_W), F32),
    )
    out_specs = (row(4 * CONV_W), pl.BlockSpec((MLA_HEADS, tm, QK_W), lambda i: (0, i, 0)),
                 row(QK_W), row(KV_LORA), row(QK_ROPE), row(MLA_W), row(SHIFT_W), row(RWKV_W))
    return pl.pallas_call(
        _proj_kernel,
        grid=(n,),
        in_specs=[row(D_MODEL)] + [full(w) for w in weights] + [tab, tab, tab],
        out_specs=out_specs,
        out_shape=out_shape,
        compiler_params=pltpu.CompilerParams(dimension_semantics=("parallel",),
                                             vmem_limit_bytes=VMEM_LIMIT),
        name="proj",
    )(x, *weights, *tabs)


def _heads_out(o_lat, wv_ref, rows):
    out = None
    for h in range(MLA_HEADS):
        part = _dot(o_lat[h * rows:(h + 1) * rows].astype(BF16), wv_ref[h])
        out = part if out is None else out + part
    return out


def _attn_prompt_kernel(q_ref, k_ref, gb_ref, wv_ref, yb_ref, s_ref, mx_ref, ls_ref, acc_ref):
    i = pl.program_id(1)
    rows = MLA_HEADS * QTILE
    q = q_ref[...].reshape(rows, QK_W)
    col = lax.broadcasted_iota(jnp.int32, (rows, QTILE), 1)
    row = lax.broadcasted_iota(jnp.int32, (rows, QTILE), 0) & (QTILE - 1)

    def key_tile(j):
        return k_ref[0, pl.ds(pl.multiple_of(j * QTILE, QTILE), QTILE), :]

    def scores(j, mask, first=False):
        s = _dot_nt(q, key_tile(j))
        if mask is not None:
            s = jnp.where(mask, s, NEG)
        s_ref[:, pl.ds(pl.multiple_of(j * QTILE, QTILE), QTILE)] = s
        mx_ref[...] = s if first else jnp.maximum(mx_ref[...], s)

    scores(0, col >= FRONT, first=True)

    def mid1(j, carry):
        scores(j, None)
        return carry

    lax.fori_loop(1, i, mid1, 0)

    @pl.when(i > 0)
    def _():
        scores(i, (col < CHUNK) | (row >= CHUNK))

    m = jnp.broadcast_to(jnp.max(mx_ref[...], axis=-1, keepdims=True), (rows, QTILE))
    mx_ref[...] = m
    ls_ref[...] = jnp.zeros((rows, QTILE), F32)
    acc_ref[...] = jnp.zeros((rows, KV_LORA), F32)

    def weigh(j, carry):
        p = jnp.exp(s_ref[:, pl.ds(pl.multiple_of(j * QTILE, QTILE), QTILE)] - mx_ref[...])
        ls_ref[...] += p
        acc_ref[...] += _dot(p.astype(BF16), key_tile(j)[:, :KV_LORA])
        return carry

    lax.fori_loop(0, i + 1, weigh, 0)

    o_lat = acc_ref[...] / jnp.sum(ls_ref[...], axis=-1, keepdims=True)
    yb_ref[...] = (_heads_out(o_lat, wv_ref, QTILE) * _silu(gb_ref[...])).astype(BF16)


def _attn_prompt(qp, kcat, gb, wv, batch, tp):
    nq = tp // QTILE
    rows = MLA_HEADS * QTILE
    return pl.pallas_call(
        _attn_prompt_kernel,
        grid=(batch, nq),
        in_specs=[
            pl.BlockSpec((MLA_HEADS, QTILE, QK_W), lambda b, i: (0, b * nq + i, 0)),
            pl.BlockSpec((1, tp, QK_W), lambda b, i: (b, 0, 0)),
            pl.BlockSpec((QTILE, MLA_W), lambda b, i: (b * nq + i, 0)),
            pl.BlockSpec(wv.shape, lambda b, i: (0, 0, 0)),
        ],
        out_specs=pl.BlockSpec((QTILE, MLA_W), lambda b, i: (b * nq + i, 0)),
        out_shape=jax.ShapeDtypeStruct((batch * tp, MLA_W), BF16),
        scratch_shapes=[pltpu.VMEM((rows, tp), F32),
                        pltpu.VMEM((rows, QTILE), F32),
                        pltpu.VMEM((rows, QTILE), F32),
                        pltpu.VMEM((rows, KV_LORA), F32)],
        compiler_params=pltpu.CompilerParams(dimension_semantics=("parallel", "arbitrary"),
                                             vmem_limit_bytes=VMEM_LIMIT),
        name="attn_prompt",
    )(qp, kcat.reshape(batch, tp, QK_W), gb, wv)


def _attn_sample_kernel(q_ref, kn_ref, ck_ref, kr_ref, gb_ref, wv_ref, yb_ref):
    ts = q_ref.shape[1]
    q = q_ref[...].reshape(MLA_HEADS * ts, QK_W)
    kn = kn_ref[...]
    ck = ck_ref[0, 0].astype(BF16)
    kr = kr_ref[0, 0].astype(BF16)
    s_past = _dot_nt(q[:, :KV_LORA], ck) + _dot_nt(q[:, KV_LORA:KV_LORA + QK_ROPE], kr)
    s_new = _dot_nt(q, kn)
    m = jnp.maximum(jnp.max(s_past, axis=-1, keepdims=True),
                    jnp.max(s_new, axis=-1, keepdims=True))
    p_past = jnp.exp(s_past - m)
    p_new = jnp.exp(s_new - m)
    l = jnp.sum(p_past, axis=-1, keepdims=True) + jnp.sum(p_new, axis=-1, keepdims=True)
    o_lat = (_dot(p_past.astype(BF16), ck) + _dot(p_new.astype(BF16), kn[:, :KV_LORA])) / l
    yb_ref[...] = (_heads_out(o_lat, wv_ref, ts) * _silu(gb_ref[...])).astype(BF16)


def _attn_sample(qp, kcat, cache_ckv, cache_krope, layer, gb, wv, batch, ts):
    past = cache_ckv.shape[2]
    return pl.pallas_call(
        _attn_sample_kernel,
        grid=(batch,),
        in_specs=[
            pl.BlockSpec((MLA_HEADS, ts, QK_W), lambda b: (0, b, 0)),
            pl.BlockSpec((ts, QK_W), lambda b: (b, 0)),
            pl.BlockSpec((1, 1, past, KV_LORA), lambda b: (layer, b, 0, 0)),
            pl.BlockSpec((1, 1, past, QK_ROPE), lambda b: (layer, b, 0, 0)),
            pl.BlockSpec((ts, MLA_W), lambda b: (b, 0)),
            pl.BlockSpec(wv.shape, lambda b: (0, 0, 0)),
        ],
        out_specs=pl.BlockSpec((ts, MLA_W), lambda b: (b, 0)),
        out_shape=jax.ShapeDtypeStruct((batch * ts, MLA_W), BF16),
        compiler_params=pltpu.CompilerParams(dimension_semantics=("parallel",),
                                             vmem_limit_bytes=VMEM_LIMIT),
        name="attn_sample",
    )(qp, kcat, cache_ckv, cache_krope, gb, wv)


def _head_sums(x, head_of_lane):
    out = jnp.zeros_like(x)
    for h in range(RWKV_HEADS):
        sel = head_of_lane == h
        s = jnp.sum(jnp.where(sel, x, 0.0), axis=-1, keepdims=True)
        out = jnp.where(sel, s, out)
    return out


def _pair_sums(x, ones2):
    hi = x.astype(BF16)
    lo = (x - hi.astype(F32)).astype(BF16)
    return _dot(jnp.concatenate([hi, lo], axis=1), ones2)


def _mix_kernel(za_ref, zc_ref, gc_ref, conv0_ref, shift0_ref, wkv0_ref,
                cw_ref, mu_ref, w0_ref, w2a2_ref, a0_ref, kkw_ref, kaw_ref, bonus_ref,
                lnw_ref, lnb_ref,
                ya_ref, yc_ref, conv_ref, shift_ref, wkv_ref,
                s_ref, cu_ref, cz_ref, vec_ref, vhl_ref, yt_ref):
    tb = pl.program_id(1)
    nblk = pl.num_programs(1)
    tblk = SCAN_T
    npair = SCAN_GROUP * RWKV_HEADS // 2
    hw = RWKV_HEAD

    @pl.when(tb == 0)
    def _():
        for pr in range(npair):
            b, hp = divmod(pr, 2)
            s_ref[pr, :, 0:hw] = wkv0_ref[b, 2 * hp]
            s_ref[pr, :, hw:2 * hw] = wkv0_ref[b, 2 * hp + 1]
        cu_ref[...] = conv0_ref[...]
        cz_ref[...] = shift0_ref[...]

    trow = lax.broadcasted_iota(jnp.int32, (tblk, 1), 0)
    head_of_lane = lax.broadcasted_iota(jnp.int32, (1, RWKV_W), 1) // RWKV_HEAD
    lane128 = lax.broadcasted_iota(jnp.int32, (1, LANES), 1)
    cw = cw_ref[...]
    bonus_v = []
    for b in range(SCAN_GROUP):
        za = za_ref[b]
        xin, bg = za[:, 0:CONV_W], za[:, CONV_W:2 * CONV_W]
        cg, ga = za[:, 2 * CONV_W:3 * CONV_W], za[:, 3 * CONV_W:4 * CONV_W]
        u = cg * xin
        cu = cu_ref[b]
        u1 = jnp.where(trow == 0, cu[1:2], pltpu.roll(u, 1, 0))
        u2 = jnp.where(trow == 0, cu[0:1], jnp.where(trow == 1, cu[1:2], pltpu.roll(u, 2, 0)))
        conv = cw[0:1] * u2 + cw[1:2] * u1 + cw[2:3] * u
        ya_ref[b] = (bg * conv * _silu(ga)).astype(BF16)
        cu_ref[b] = u[tblk - (CONV_K - 1):tblk]

        zc = zc_ref[b]
        prev = jnp.where(trow == 0, cz_ref[b], pltpu.roll(zc, 1, 0))
        cz_ref[b] = zc[tblk - 1:tblk]
        zs = zc + (prev - zc) * mu_ref[...]
        r, k, v = zs[:, 0:RWKV_W], zs[:, RWKV_W:2 * RWKV_W], zs[:, 2 * RWKV_W:3 * RWKV_W]
        la = zs[:, 3 * RWKV_W:SHIFT_W]
        la = jnp.where(lane128 < DECAY_LORA, jnp.tanh(la), la)
        lora = _dot(la.astype(BF16), w2a2_ref[...])
        dec_in = w0_ref[...] + lora[:, 0:RWKV_W]
        w_log = -(jnp.maximum(-dec_in, 0.0) + jnp.log1p(jnp.exp(-jnp.abs(dec_in)))) - 0.5
        decay = jnp.exp(-jnp.exp(w_log))
        a = jax.nn.sigmoid(a0_ref[...] + lora[:, RWKV_W:2 * RWKV_W])
        kk = k * kkw_ref[...]
        kk = kk / jnp.maximum(jnp.sqrt(_head_sums(kk * kk, head_of_lane)), 1e-12)
        kp = k * (1.0 + (a - 1.0) * kaw_ref[...])
        bonus_v.append(_head_sums(r * kp * bonus_ref[...], head_of_lane) * v)

        for n, vec in enumerate((-kk, decay, kk * a, kp, r)):
            vec_ref[n, 2 * b] = vec[:, 0:LANES]
            vec_ref[n, 2 * b + 1] = vec[:, LANES:2 * LANES]
        v_hi = v.astype(BF16).astype(F32)
        v_lo = v - v_hi
        for hp in range(2):
            vhl_ref[0, 2 * b + hp] = v_hi[:, hp * LANES:(hp + 1) * LANES]
            vhl_ref[1, 2 * b + hp] = v_lo[:, hp * LANES:(hp + 1) * LANES]

    ri = lax.broadcasted_iota(jnp.int32, (2 * LANES, LANES), 0)
    ci = lax.broadcasted_iota(jnp.int32, (2 * LANES, LANES), 1)
    ones2 = jnp.where(((ri >> 6) & 1) == (ci >> 6), 1.0, 0.0).astype(BF16)
    eye = jnp.where(lax.broadcasted_iota(jnp.int32, (hw, LANES), 0)
                    == (lax.broadcasted_iota(jnp.int32, (hw, LANES), 1) & (hw - 1)), 1.0, 0.0)
    tok_of_lane = lane128 & (SCAN_T - 1)

    def step(t, carry):
        def rowv(n, pr):
            return vec_ref[n, pr, pl.ds(t, 1), :]

        def vrow(n, pr):
            return vhl_ref[n, pr, pl.ds(t, 1), :]

        s_old = [s_ref[pr] for pr in range(npair)]
        sa = _pair_sums(jnp.concatenate([s_old[pr] * rowv(0, pr) for pr in range(npair)], axis=0),
                        ones2)
        diag = jnp.concatenate(
            [jnp.concatenate([(eye * vrow(0, pr)).astype(BF16), (eye * vrow(1, pr)).astype(BF16)],
                             axis=1) for pr in range(npair)], axis=0)
        vb = _dot(diag, ones2)
        sr = []
        for pr in range(npair):
            rows = slice(pr * hw, (pr + 1) * hw)
            s = s_old[pr] * rowv(1, pr) + sa[rows] * rowv(2, pr) + vb[rows] * rowv(3, pr)
            s_ref[pr] = s
            sr.append(s * rowv(4, pr))
        y = _pair_sums(jnp.concatenate(sr, axis=0), ones2)
        hit = tok_of_lane == t
        for pr in range(npair):
            yt_ref[pr] = jnp.where(hit, y[pr * hw:(pr + 1) * hw], yt_ref[pr])
        return carry

    yt_ref[...] = jnp.zeros(yt_ref.shape, F32)
    lax.fori_loop(0, SCAN_T, step, 0)

    zpad = jnp.zeros((LANES - hw, LANES), F32)
    for b in range(SCAN_GROUP):
        halves = []
        for hp in range(2):
            z = jnp.concatenate([yt_ref[2 * b + hp], zpad], axis=0).T
            halves.append(z[0:SCAN_T] + pltpu.roll(z[SCAN_T:2 * SCAN_T], hw, 1))
        ys = jnp.concatenate(halves, axis=1)
        mean = _head_sums(ys, head_of_lane) * (1.0 / RWKV_HEAD)
        d = ys - mean
        var = _head_sums(d * d, head_of_lane) * (1.0 / RWKV_HEAD)
        yn = d * lax.rsqrt(var + GN_EPS) * lnw_ref[...] + lnb_ref[...]
        yn = yn + bonus_v[b]
        yc_ref[b] = (yn * _silu(gc_ref[b])).astype(BF16)

    @pl.when(tb == nblk - 1)
    def _():
        conv_ref[...] = cu_ref[...]
        shift_ref[...] = cz_ref[...]
        for pr in range(npair):
            b, hp = divmod(pr, 2)
            wkv_ref[b, 2 * hp] = s_ref[pr, :, 0:hw]
            wkv_ref[b, 2 * hp + 1] = s_ref[pr, :, hw:2 * hw]


def _mix(za, zc, gc, conv0, shift0, wkv0, lw, batch, t):
    groups = batch // SCAN_GROUP
    nblk = t // SCAN_T
    npair = SCAN_GROUP * RWKV_HEADS // 2
    seq = lambda w: pl.BlockSpec((SCAN_GROUP, SCAN_T, w), lambda g, i: (g, i, 0))
    state3 = lambda a: pl.BlockSpec((SCAN_GROUP,) + a.shape[1:], lambda g, i: (g, 0, 0))
    state4 = pl.BlockSpec((SCAN_GROUP, RWKV_HEADS, RWKV_HEAD, RWKV_HEAD),
                          lambda g, i: (g, 0, 0, 0))
    full = lambda a: pl.BlockSpec(a.shape, lambda g, i: (0,) * a.ndim)
    params = (lw["conv_w"], lw["shift_mu"], lw["decay_w0"], lw["w2a2"], lw["iclr_a0"],
              lw["key_kk"], lw["key_ka"], lw["bonus_rk"], lw["lnx_w"], lw["lnx_b"])
    out_shape = (
        jax.ShapeDtypeStruct((batch, t, CONV_W), BF16),
        jax.ShapeDtypeStruct((batch, t, RWKV_W), BF16),
        jax.ShapeDtypeStruct(conv0.shape, F32),
        jax.ShapeDtypeStruct(shift0.shape, F32),
        jax.ShapeDtypeStruct(wkv0.shape, F32),
    )
    return pl.pallas_call(
        _mix_kernel,
        grid=(groups, nblk),
        in_specs=[seq(4 * CONV_W), seq(SHIFT_W), seq(RWKV_W), state3(conv0), state3(shift0),
                  state4] + [full(p) for p in params],
        out_specs=(seq(CONV_W), seq(RWKV_W), state3(conv0), state3(shift0), state4),
        out_shape=out_shape,
        scratch_shapes=[
            pltpu.VMEM((npair, RWKV_HEAD, LANES), F32),
            pltpu.VMEM((SCAN_GROUP, CONV_K - 1, CONV_W), F32),
            pltpu.VMEM((SCAN_GROUP, 1, SHIFT_W), F32),
            pltpu.VMEM((5, npair, SCAN_T, LANES), F32),
            pltpu.VMEM((2, npair, SCAN_T, LANES), F32),
            pltpu.VMEM((npair, RWKV_HEAD, LANES), F32),
        ],
        compiler_params=pltpu.CompilerParams(dimension_semantics=("parallel", "arbitrary"),
                                             vmem_limit_bytes=VMEM_LIMIT),
        name="mix",
    )(za, zc, gc, conv0, shift0, wkv0, *params)


def _out_kernel(ya_ref, yb_ref, yc_ref, x_ref, wa_ref, wb_ref, wc_ref, fg_ref, o_ref, *, final):
    y = (_dot(ya_ref[...], wa_ref[...]) + _dot(yb_ref[...], wb_ref[...])
         + _dot(yc_ref[...], wc_ref[...]))
    y = x_ref[...] + y
    if final:
        y = _rms(y, fg_ref[...])
    o_ref[...] = y


def _out(ya, yb, yc, x, lw, final_g, final, tm):
    rows = x.shape[0]
    row = lambda w: pl.BlockSpec((tm, w), lambda i: (i, 0))
    full = lambda a: pl.BlockSpec(a.shape, lambda i: (0,) * a.ndim)
    weights = (lw["wo_a"], lw["wo_b"], lw["wo_c"], final_g)
    return pl.pallas_call(
        functools.partial(_out_kernel, final=final),
        grid=(rows // tm,),
        in_specs=[row(CONV_W), row(MLA_W), row(RWKV_W), row(D_MODEL)] + [full(w) for w in weights],
        out_specs=row(D_MODEL),
        out_shape=jax.ShapeDtypeStruct((rows, D_MODEL), F32),
        compiler_params=pltpu.CompilerParams(dimension_semantics=("parallel",),
                                             vmem_limit_bytes=VMEM_LIMIT),
        name="out",
    )(ya, yb, yc, x, *weights)


def _layer_weights(l, norm_g, w_in, conv_w, q_norm_g, w_uq, kv_norm_g, w_ukv, shift_mu,
                   decay_w0, decay_w2, iclr_a0, iclr_a2, key_kk, key_ka, bonus_rk, lnx_w, lnx_b,
                   w_out):
    row = lambda p: p[l].reshape(1, -1).astype(F32)
    wi = w_in[l].astype(BF16)
    o = np.cumsum([0, 4 * CONV_W, Q_LORA, KV_LORA, QK_ROPE, MLA_W, SHIFT_W, RWKV_W]).tolist()
    cols = lambda j: wi[:, o[j]:o[j + 1]]
    uq = w_uq[l].astype(BF16).reshape(Q_LORA, MLA_HEADS, QK_NOPE + QK_ROPE)
    uq_rope = jnp.pad(uq[:, :, QK_NOPE:], ((0, 0), (0, 0), (0, LANES - QK_ROPE)))
    ukv = w_ukv[l].astype(BF16).reshape(KV_LORA, MLA_HEADS, QK_NOPE + V_DIM)
    eye = jnp.eye(MLA_HEADS, dtype=BF16)
    bd = jnp.einsum("chd,hg->hdgc", ukv[:, :, :QK_NOPE], eye).reshape(
        MLA_HEADS * QK_NOPE, MLA_HEADS * KV_LORA)
    wv = jnp.einsum("che,hg->hcge", ukv[:, :, QK_NOPE:], eye).reshape(
        MLA_HEADS, KV_LORA, MLA_W)
    zeros = jnp.zeros((DECAY_LORA, RWKV_W), BF16)
    w2a2 = jnp.concatenate([
        jnp.concatenate([decay_w2[l].astype(BF16), zeros], axis=1),
        jnp.concatenate([zeros, iclr_a2[l].astype(BF16)], axis=1)], axis=0)
    wo = w_out[l].astype(BF16)
    return {
        "norm_g": row(norm_g), "wa": cols(0), "wq": cols(1), "wckv": cols(2),
        "wkr": jnp.pad(cols(3), ((0, 0), (0, LANES - QK_ROPE))),
        "wgb": cols(4), "wzc": cols(5), "wgc": cols(6),
        "q_norm_g": row(q_norm_g),
        "wuq": jnp.concatenate([uq[:, :, :QK_NOPE].reshape(Q_LORA, -1),
                                uq_rope.reshape(Q_LORA, -1)], axis=1),
        "bd": bd, "wv": wv, "kv_norm_g": row(kv_norm_g),
        "conv_w": conv_w[l].astype(F32), "shift_mu": row(shift_mu), "decay_w0": row(decay_w0),
        "w2a2": w2a2, "iclr_a0": row(iclr_a0), "key_kk": row(key_kk), "key_ka": row(key_ka),
        "bonus_rk": row(bonus_rk), "lnx_w": row(lnx_w), "lnx_b": row(lnx_b),
        "wo_a": wo[0:CONV_W], "wo_b": wo[CONV_W:CONV_W + MLA_W], "wo_c": wo[CONV_W + MLA_W:],
    }


def _rope_tables(pos):
    half = QK_ROPE // 2
    inv = ROPE_BASE ** (-jnp.arange(half, dtype=F32) * 2.0 / QK_ROPE)
    ang = pos.astype(F32)[:, None] * inv[None, :]
    cos, sin = jnp.cos(ang), jnp.sin(ang)
    z = jnp.zeros((pos.shape[0], LANES - QK_ROPE), F32)
    zh = jnp.zeros((pos.shape[0], half), F32)
    return (jnp.concatenate([cos, cos, z], axis=1),
            jnp.concatenate([-sin, zh, z], axis=1),
            jnp.concatenate([zh, sin, z], axis=1))


def _stream(x, batch, t, tm, tabs, tab_blocks, states, attn, lws, final_g):
    outs = []
    depth = len(lws)
    for l, lw in enumerate(lws):
        za, qp, kcat, ckv, kr, gb, zc, gc = _proj(x, lw, tabs, tm, tab_blocks)
        yb = attn(l, lw, qp, kcat, gb)
        conv0, shift0, wkv0 = states[l]
        ya, yc, conv, shift, wkv = _mix(
            za.reshape(batch, t, -1), zc.reshape(batch, t, -1), gc.reshape(batch, t, -1),
            conv0, shift0, wkv0, lw, batch, t)
        x = _out(ya.reshape(batch * t, -1), yb, yc.reshape(batch * t, -1), x, lw, final_g,
                 l == depth - 1, tm)
        outs.append((ckv.reshape(batch, t, -1), kr.reshape(batch, t, -1), conv,
                     shift.reshape(batch, -1), wkv))
    return x, outs


def kernel(x_prompt, x_sample, cache_ckv, cache_krope, state_conv, state_shift, state_wkv,
           meta_tokens, norm_g, w_in, conv_w, q_norm_g, w_uq, kv_norm_g, w_ukv, shift_mu,
           decay_w0, decay_w2, iclr_a0, iclr_a2, key_kk, key_ka, bonus_rk, lnx_w, lnx_b,
           w_out, final_g):
    depth = w_in.shape[0]
    bp, seq = x_prompt.shape[:2]
    bs, ts = x_sample.shape[:2]
    past = cache_ckv.shape[2]
    tp = FRONT + N_META + seq
    assert tp % (3 * QTILE) == 0 and bp % SCAN_GROUP == 0 and bs % SCAN_GROUP == 0
    lws = [_layer_weights(l, norm_g, w_in, conv_w, q_norm_g, w_uq, kv_norm_g, w_ukv, shift_mu,
                          decay_w0, decay_w2, iclr_a0, iclr_a2, key_kk, key_ka, bonus_rk,
                          lnx_w, lnx_b, w_out) for l in range(depth)]
    fg = final_g.reshape(1, -1).astype(F32)

    xp = jnp.concatenate([
        jnp.zeros((bp, FRONT, D_MODEL), F32),
        jnp.broadcast_to(meta_tokens.astype(F32)[None], (bp, N_META, D_MODEL)),
        x_prompt], axis=1).reshape(bp * tp, D_MODEL)
    tm_p = 3 * QTILE
    tabs_p = _rope_tables(jnp.maximum(jnp.arange(tp, dtype=jnp.int32) - FRONT, 0))
    zero_states = [(jnp.zeros((bp, CONV_K - 1, CONV_W), F32), jnp.zeros((bp, 1, SHIFT_W), F32),
                    jnp.zeros((bp, RWKV_HEADS, RWKV_HEAD, RWKV_HEAD), F32))] * depth
    attn_p = lambda l, lw, qp, kcat, gb: _attn_prompt(qp, kcat, gb, lw["wv"], bp, tp)
    yp, outs_p = _stream(xp, bp, tp, tm_p, tabs_p, tp // tm_p, zero_states, attn_p, lws, fg)

    tm_s = 512
    reps = tm_s // ts
    tabs_s = tuple(jnp.tile(tb_, (reps, 1))
                   for tb_ in _rope_tables(past + jnp.arange(ts, dtype=jnp.int32)))
    states_s = [(state_conv[l], state_shift[l].reshape(bs, 1, SHIFT_W), state_wkv[l])
                for l in range(depth)]
    attn_s = lambda l, lw, qp, kcat, gb: _attn_sample(qp, kcat, cache_ckv, cache_krope, l, gb,
                                                      lw["wv"], bs, ts)
    ys, outs_s = _stream(x_sample.reshape(bs * ts, D_MODEL), bs, ts, tm_s, tabs_s, 1,
                         states_s, attn_s, lws, fg)

    stack = lambda outs, j: jnp.stack([o[j] for o in outs])
    y_prompt = yp.reshape(bp, tp, D_MODEL)[:, FRONT + N_META:]
    ckv_p = stack(outs_p, 0)[:, :, FRONT:]
    kr_p = stack(outs_p, 1)[:, :, FRONT:]
    return (y_prompt, ys.reshape(bs, ts, D_MODEL), ckv_p, kr_p, stack(outs_p, 2),
            stack(outs_p, 3), stack(outs_p, 4),
            stack(outs_s, 0), stack(outs_s, 1), stack(outs_s, 2), stack(outs_s, 3),
            stack(outs_s, 4))
```

```python
import functools

import jax
import jax.numpy as jnp
import numpy as np
from jax import lax
from jax.experimental import pallas as pl
from jax.experimental.pallas import tpu as pltpu

D_MODEL = 1024
CHUNK = 64
N_META = 16
CONV_W = 256
CONV_K = 3
MLA_HEADS = 8
QK_NOPE = 64
QK_ROPE = 32
V_DIM = 64
MLA_W = MLA_HEADS * V_DIM
Q_LORA = 256
KV_LORA = 128
RWKV_HEAD = 64
RWKV_W = 256
RWKV_HEADS = 4
DECAY_LORA = 64
ICLR_LORA = 64
SHIFT_W = 3 * RWKV_W + DECAY_LORA + ICLR_LORA
ROPE_BASE = 10000.0
RMS_EPS = 1e-6
GN_EPS = 64e-5
NEG = -1e30
LOG2_E = 1.4426950408889634

LANES = 128
QTILE = 128
KTILE = 256
FRONT = KTILE - N_META
QK_W = 2 * LANES
PROMPT_ROW_TILES = 8
SCAN_GROUP = 4
SCAN_T = 64
SCAN_SPLIT = 2
VMEM_LIMIT = 48 * 1024 * 1024

F32 = jnp.float32
BF16 = jnp.bfloat16


def _dot(a, b):
    return jnp.dot(a, b, preferred_element_type=F32)


def _dot_nt(a, b):
    return lax.dot_general(a, b, (((1,), (1,)), ((), ())), preferred_element_type=F32)


def _rms(x, g):
    return x * lax.rsqrt(jnp.mean(x * x, axis=-1, keepdims=True) + RMS_EPS) * g


def _silu(x):
    return x * jax.nn.sigmoid(x)


def _rope_block(x, cos, sin_lo, sin_hi):
    return (x * cos + pltpu.roll(x, LANES - QK_ROPE // 2, 1) * sin_lo
            + pltpu.roll(x, QK_ROPE // 2, 1) * sin_hi)


def _proj_kernel(x_ref, g_ref, wa_ref, wq_ref, wckv_ref, wkr_ref, wgb_ref, wzc_ref, wgc_ref,
                 qg_ref, wuq_ref, bd_ref, kvg_ref, cos_ref, slo_ref, shi_ref,
                 za_ref, qp_ref, kcat_ref, ckv_ref, kr_ref, gb_ref, zc_ref, gc_ref):
    scale = (QK_NOPE + QK_ROPE) ** -0.5 * LOG2_E
    hb = _rms(x_ref[...], g_ref[...]).astype(BF16)
    za_ref[...] = _dot(hb, wa_ref[...])
    gb_ref[...] = _dot(hb, wgb_ref[...])
    zc_ref[...] = _dot(hb, wzc_ref[...])
    gc_ref[...] = _dot(hb, wgc_ref[...])

    cos, slo, shi = cos_ref[...], slo_ref[...], shi_ref[...]
    cqn = _rms(_dot(hb, wq_ref[...]), qg_ref[...]).astype(BF16)
    qe = _dot(cqn, wuq_ref[...])
    nope_w = MLA_HEADS * QK_NOPE
    qabs = _dot(qe[:, :nope_w].astype(BF16), bd_ref[...])
    for h in range(MLA_HEADS):
        lo, hi = h * LANES, (h + 1) * LANES
        rot = _rope_block(qe[:, nope_w + lo:nope_w + hi], cos, slo, shi)
        qp_ref[h, :, 0:LANES] = (qabs[:, lo:hi] * scale).astype(BF16)
        qp_ref[h, :, LANES:QK_W] = (rot * scale).astype(BF16)

    ckvn = _rms(_dot(hb, wckv_ref[...]), kvg_ref[...])
    ckv_ref[...] = ckvn
    krr = _rope_block(_dot(hb, wkr_ref[...]), cos, slo, shi)
    kr_ref[...] = krr[:, :QK_ROPE]
    kcat_ref[:, 0:LANES] = ckvn.astype(BF16)
    kcat_ref[:, LANES:QK_W] = krr.astype(BF16)


def _proj(x, lw, tabs, tm, tab_blocks):
    rows = x.shape[0]
    n = rows // tm
    row = lambda w: pl.BlockSpec((tm, w), lambda i: (i, 0))
    full = lambda a: pl.BlockSpec(a.shape, lambda i: (0,) * a.ndim)
    tab = pl.BlockSpec((tm, LANES), lambda i: (i % tab_blocks, 0))
    weights = (lw["norm_g"], lw["wa"], lw["wq"], lw["wckv"], lw["wkr"], lw["wgb"], lw["wzc"],
               lw["wgc"], lw["q_norm_g"], lw["wuq"], lw["bd"], lw["kv_norm_g"])
    out_shape = (
        jax.ShapeDtypeStruct((rows, 4 * CONV_W), F32),
        jax.ShapeDtypeStruct((MLA_HEADS, rows, QK_W), BF16),
        jax.ShapeDtypeStruct((rows, QK_W), BF16),
        jax.ShapeDtypeStruct((rows, KV_LORA), F32),
        jax.ShapeDtypeStruct((rows, QK_ROPE), F32),
        jax.ShapeDtypeStruct((rows, MLA_W), F32),
        jax.ShapeDtypeStruct((rows, SHIFT_W), F32),
        jax.ShapeDtypeStruct((rows, RWKV_W), F32),
    )
    out_specs = (row(4 * CONV_W), pl.BlockSpec((MLA_HEADS, tm, QK_W), lambda i: (0, i, 0)),
                 row(QK_W), row(KV_LORA), row(QK_ROPE), row(MLA_W), row(SHIFT_W), row(RWKV_W))
    return pl.pallas_call(
        _proj_kernel,
        grid=(n,),
        in_specs=[row(D_MODEL)] + [full(w) for w in weights] + [tab, tab, tab],
        out_specs=out_specs,
        out_shape=out_shape,
        compiler_params=pltpu.CompilerParams(dimension_semantics=("parallel",),
                                             vmem_limit_bytes=VMEM_LIMIT),
        name="proj",
    )(x, *weights, *tabs)


def _heads_out(o_lat, wv_ref, rows):
    out = None
    for h in range(MLA_HEADS):
        part = _dot(o_lat[h * rows:(h + 1) * rows].astype(BF16), wv_ref[h])
        out = part if out is None else out + part
    return out


def _attn_prompt_kernel(q_ref, k_ref, gb_ref, wv_ref, yb_ref, s_ref, mx_ref, ls_ref, acc_ref):
    i = pl.program_id(1)
    rows = MLA_HEADS * QTILE

    @pl.when(i == 0)
    def _():
        yb_ref[...] = jnp.zeros(yb_ref.shape, BF16)

    @pl.when(i > 0)
    def _():
        q = q_ref[...].reshape(rows, QK_W)
        col = lax.broadcasted_iota(jnp.int32, (rows, KTILE), 1)
        qrow = (lax.broadcasted_iota(jnp.int32, (rows, KTILE), 0) & (QTILE - 1)) + i * QTILE
        last = i >> 1

        def keys(j, width):
            return k_ref[0, pl.ds(pl.multiple_of(j * KTILE, KTILE), width), :]

        def s_at(j, width):
            return s_ref.at[:, pl.ds(pl.multiple_of(j * KTILE, KTILE), width)]

        def fold(x, op):
            out = x[:, :LANES]
            for c in range(1, x.shape[1] // LANES):
                out = op(out, x[:, c * LANES:(c + 1) * LANES])
            return out

        def scores(j, width, mask=None, first=False):
            s = _dot_nt(q, keys(j, width))
            if mask is not None:
                s = jnp.where(mask, s, NEG)
            s_at(j, width)[...] = s
            sm = fold(s, jnp.maximum)
            mx_ref[...] = sm if first else jnp.maximum(mx_ref[...], sm)

        scores(0, KTILE, col >= FRONT, first=True)
        inner = jnp.maximum(last - 1, 0)

        def wide1(jj, carry):
            scores(1 + 2 * jj, 2 * KTILE)
            return carry

        lax.fori_loop(0, inner >> 1, wide1, 0)

        @pl.when((inner & 1) == 1)
        def _():
            scores(last - 1, KTILE)

        @pl.when(last > 0)
        def _():
            scores(last, KTILE, ((col + last * KTILE) >> 6) <= (qrow >> 6))

        m = jnp.broadcast_to(jnp.max(mx_ref[...], axis=-1, keepdims=True), (rows, LANES))
        mx_ref[...] = m
        ls_ref[...] = jnp.zeros((rows, LANES), F32)
        acc_ref[...] = jnp.zeros((rows, KV_LORA), F32)

        def weigh(j, width):
            s = s_at(j, width)[...]
            m_ = mx_ref[...]
            p = [jnp.exp2(s[:, c * LANES:(c + 1) * LANES] - m_) for c in range(width // LANES)]
            ls_ref[...] += functools.reduce(lambda a, b: a + b, p)
            pb = jnp.concatenate([x.astype(BF16) for x in p], axis=1)
            acc_ref[...] += _dot(pb, keys(j, width)[:, :KV_LORA])

        def wide2(jj, carry):
            weigh(2 * jj, 2 * KTILE)
            return carry

        ntile = last + 1
        lax.fori_loop(0, ntile >> 1, wide2, 0)

        @pl.when((ntile & 1) == 1)
        def _():
            weigh(last, KTILE)

        o_lat = acc_ref[...] / jnp.sum(ls_ref[...], axis=-1, keepdims=True)
        yb_ref[...] = (_heads_out(o_lat, wv_ref, QTILE) * _silu(gb_ref[...])).astype(BF16)


def _attn_prompt(qp, kcat, gb, wv, batch, tp):
    nq = tp // QTILE
    rows = MLA_HEADS * QTILE
    return pl.pallas_call(
        _attn_prompt_kernel,
        grid=(batch, nq),
        in_specs=[
            pl.BlockSpec((MLA_HEADS, QTILE, QK_W), lambda b, i: (0, b * nq + i, 0)),
            pl.BlockSpec((1, tp, QK_W), lambda b, i: (b, 0, 0)),
            pl.BlockSpec((QTILE, MLA_W), lambda b, i: (b * nq + i, 0)),
            pl.BlockSpec(wv.shape, lambda b, i: (0, 0, 0)),
        ],
        out_specs=pl.BlockSpec((QTILE, MLA_W), lambda b, i: (b * nq + i, 0)),
        out_shape=jax.ShapeDtypeStruct((batch * tp, MLA_W), BF16),
        scratch_shapes=[pltpu.VMEM((rows, tp), F32),
                        pltpu.VMEM((rows, LANES), F32),
                        pltpu.VMEM((rows, LANES), F32),
                        pltpu.VMEM((rows, KV_LORA), F32)],
        compiler_params=pltpu.CompilerParams(dimension_semantics=("parallel", "arbitrary"),
                                             vmem_limit_bytes=VMEM_LIMIT),
        name="attn_prompt",
    )(qp, kcat.reshape(batch, tp, QK_W), gb, wv)


def _attn_sample_kernel(q_ref, kn_ref, ck_ref, kr_ref, gb_ref, wv_ref, yb_ref):
    ts = q_ref.shape[1]
    q = q_ref[...].reshape(MLA_HEADS * ts, QK_W)
    kn = kn_ref[...]
    ck = ck_ref[0, 0].astype(BF16)
    kr = kr_ref[0, 0].astype(BF16)
    s_past = _dot_nt(q[:, :KV_LORA], ck) + _dot_nt(q[:, KV_LORA:KV_LORA + QK_ROPE], kr)
    s_new = _dot_nt(q, kn)
    m = jnp.maximum(jnp.max(s_past, axis=-1, keepdims=True),
                    jnp.max(s_new, axis=-1, keepdims=True))
    p_past = jnp.exp2(s_past - m)
    p_new = jnp.exp2(s_new - m)
    l = jnp.sum(p_past, axis=-1, keepdims=True) + jnp.sum(p_new, axis=-1, keepdims=True)
    o_lat = (_dot(p_past.astype(BF16), ck) + _dot(p_new.astype(BF16), kn[:, :KV_LORA])) / l
    yb_ref[...] = (_heads_out(o_lat, wv_ref, ts) * _silu(gb_ref[...])).astype(BF16)


def _attn_sample(qp, kcat, cache_ckv, cache_krope, layer, gb, wv, batch, ts):
    past = cache_ckv.shape[2]
    return pl.pallas_call(
        _attn_sample_kernel,
        grid=(batch,),
        in_specs=[
            pl.BlockSpec((MLA_HEADS, ts, QK_W), lambda b: (0, b, 0)),
            pl.BlockSpec((ts, QK_W), lambda b: (b, 0)),
            pl.BlockSpec((1, 1, past, KV_LORA), lambda b: (layer, b, 0, 0)),
            pl.BlockSpec((1, 1, past, QK_ROPE), lambda b: (layer, b, 0, 0)),
            pl.BlockSpec((ts, MLA_W), lambda b: (b, 0)),
            pl.BlockSpec(wv.shape, lambda b: (0, 0, 0)),
        ],
        out_specs=pl.BlockSpec((ts, MLA_W), lambda b: (b, 0)),
        out_shape=jax.ShapeDtypeStruct((batch * ts, MLA_W), BF16),
        compiler_params=pltpu.CompilerParams(dimension_semantics=("parallel",),
                                             vmem_limit_bytes=VMEM_LIMIT),
        name="attn_sample",
    )(qp, kcat, cache_ckv, cache_krope, gb, wv)


def _head_sums(x, head_of_lane):
    out = jnp.zeros_like(x)
    for h in range(RWKV_HEADS):
        sel = head_of_lane == h
        s = jnp.sum(jnp.where(sel, x, 0.0), axis=-1, keepdims=True)
        out = jnp.where(sel, s, out)
    return out


def _pair_sums(x, ones2):
    hi = x.astype(BF16)
    lo = (x - hi.astype(F32)).astype(BF16)
    return _dot(jnp.concatenate([hi, lo], axis=1), ones2)


def _mix_kernel(za_ref, zc_ref, gc_ref, conv0_ref, shift0_ref, wkv0_ref,
                cw_ref, mu_ref, w0_ref, w2a2_ref, a0_ref, kkw_ref, kaw_ref, bonus_ref,
                lnw_ref, lnb_ref, wtab_ref,
                ya_ref, yc_ref, conv_ref, shift_ref, wkv_ref,
                s_ref, cu_ref, cz_ref, vec_ref, vtl_ref, yt_ref, *, nblk):
    tb = pl.program_id(1)
    tblk = SCAN_T
    npair = SCAN_GROUP * RWKV_HEADS // 2
    hw = RWKV_HEAD

    @pl.when(tb == 0)
    def _():
        for pr in range(npair):
            b, hp = divmod(pr, 2)
            s_ref[pr, :, 0:hw] = wkv0_ref[b, 2 * hp]
            s_ref[pr, :, hw:2 * hw] = wkv0_ref[b, 2 * hp + 1]
        cu_ref[...] = conv0_ref[...]
        cz_ref[...] = shift0_ref[...]

    trow = lax.broadcasted_iota(jnp.int32, (tblk, 1), 0)
    head_of_lane = lax.broadcasted_iota(jnp.int32, (1, RWKV_W), 1) // RWKV_HEAD
    lane128 = lax.broadcasted_iota(jnp.int32, (1, LANES), 1)
    cw = cw_ref[...]
    zpad = jnp.zeros((LANES - hw, LANES), F32)
    bonus_v = []
    for b in range(SCAN_GROUP):
        za = za_ref[b]
        xin, bg = za[:, 0:CONV_W], za[:, CONV_W:2 * CONV_W]
        cg, ga = za[:, 2 * CONV_W:3 * CONV_W], za[:, 3 * CONV_W:4 * CONV_W]
        u = cg * xin
        cu = cu_ref[b]
        u1 = jnp.where(trow == 0, cu[1:2], pltpu.roll(u, 1, 0))
        u2 = jnp.where(trow == 0, cu[0:1], jnp.where(trow == 1, cu[1:2], pltpu.roll(u, 2, 0)))
        conv = cw[0:1] * u2 + cw[1:2] * u1 + cw[2:3] * u
        ya_ref[b] = (bg * conv * _silu(ga)).astype(BF16)
        cu_ref[b] = u[tblk - (CONV_K - 1):tblk]

        zc = zc_ref[b]
        prev = jnp.where(trow == 0, cz_ref[b], pltpu.roll(zc, 1, 0))
        cz_ref[b] = zc[tblk - 1:tblk]
        zs = zc + (prev - zc) * mu_ref[...]
        r, k, v = zs[:, 0:RWKV_W], zs[:, RWKV_W:2 * RWKV_W], zs[:, 2 * RWKV_W:3 * RWKV_W]
        la = zs[:, 3 * RWKV_W:SHIFT_W]
        la = jnp.where(lane128 < DECAY_LORA, jnp.tanh(la), la)
        lora = _dot(la.astype(BF16), w2a2_ref[...])
        dec_in = w0_ref[...] + lora[:, 0:RWKV_W]
        w_log = -(jnp.maximum(-dec_in, 0.0) + jnp.log1p(jnp.exp(-jnp.abs(dec_in)))) - 0.5
        decay = jnp.exp(-jnp.exp(w_log))
        a = jax.nn.sigmoid(a0_ref[...] + lora[:, RWKV_W:2 * RWKV_W])
        kk = k * kkw_ref[...]
        kk = kk / jnp.maximum(jnp.sqrt(_head_sums(kk * kk, head_of_lane)), 1e-12)
        kp = k * (1.0 + (a - 1.0) * kaw_ref[...])
        bonus_v.append(_head_sums(r * kp * bonus_ref[...], head_of_lane) * v)

        for n, vec in enumerate((-kk, decay, kk * a, kp, r)):
            vec_ref[n, 2 * b] = vec[:, 0:LANES]
            vec_ref[n, 2 * b + 1] = vec[:, LANES:2 * LANES]
        for hp in range(2):
            z = jnp.concatenate([v[:, hp * LANES:(hp + 1) * LANES], zpad], axis=0).T
            vt = z[0:hw] + pltpu.roll(z[hw:2 * hw], SCAN_T, 1)
            vt_hi = vt.astype(BF16)
            vt_lo = (vt - vt_hi.astype(F32)).astype(BF16)
            pr = 2 * b + hp
            vtl_ref[pr * hw:(pr + 1) * hw, :] = jnp.concatenate([vt_hi, vt_lo], axis=1)

    ri = lax.broadcasted_iota(jnp.int32, (2 * LANES, 2 * LANES), 0)
    ci = lax.broadcasted_iota(jnp.int32, (2 * LANES, 2 * LANES), 1)
    ones4 = jnp.where((ri >> 6) == (ci >> 6), 1.0, 0.0).astype(BF16)
    ones2 = jnp.where(((ri >> 6) & 1) == (ci >> 6), 1.0, 0.0).astype(BF16)[:, 0:LANES]
    tok_of_lane = lane128 & (SCAN_T - 1)
    gsz = npair // SCAN_SPLIT
    half = gsz // 2

    def readout(g, s_list, t_row, t_hit):
        prs = list(range(g * gsz, (g + 1) * gsz))
        sr = [(s * vec_ref[4, pr, pl.ds(t_row, 1), :]).astype(BF16) for s, pr in zip(s_list, prs)]
        y = _dot(jnp.concatenate([jnp.concatenate(sr[:half], axis=0),
                                  jnp.concatenate(sr[half:], axis=0)], axis=1), ones4)
        hit = tok_of_lane == t_hit
        for n, pr in enumerate(prs):
            blk = y[(n % half) * hw:(n % half + 1) * hw,
                    (n // half) * LANES:(n // half + 1) * LANES]
            yt_ref[pr] = jnp.where(hit, blk, yt_ref[pr])

    def step(t, carry):
        rowv = lambda n, pr: vec_ref[n, pr, pl.ds(t, 1), :]
        groups = [list(range(g * gsz, (g + 1) * gsz)) for g in range(SCAN_SPLIT)]
        wt = wtab_ref[t]
        vb = [_dot(vtl_ref[g * gsz * hw:(g + 1) * gsz * hw, :], wt) for g in range(SCAN_SPLIT)]
        s_old = [[s_ref[pr] for pr in prs] for prs in groups]
        sa = [_pair_sums(jnp.concatenate([s * rowv(0, pr) for s, pr in zip(s_old[g], prs)],
                                         axis=0), ones2) for g, prs in enumerate(groups)]
        for g, prs in enumerate(groups):
            for n, pr in enumerate(prs):
                rows = slice(n * hw, (n + 1) * hw)
                s_ref[pr] = (s_old[g][n] * rowv(1, pr) + sa[g][rows] * rowv(2, pr)
                             + vb[g][rows] * rowv(3, pr))
        for g in range(SCAN_SPLIT):
            readout(g, s_old[g], jnp.maximum(t - 1, 0), t - 1)
        return carry

    yt_ref[...] = jnp.zeros(yt_ref.shape, F32)
    lax.fori_loop(0, SCAN_T, step, 0)
    for g in range(SCAN_SPLIT):
        readout(g, [s_ref[pr] for pr in range(g * gsz, (g + 1) * gsz)], SCAN_T - 1, SCAN_T - 1)

    for b in range(SCAN_GROUP):
        halves = []
        for hp in range(2):
            z = jnp.concatenate([yt_ref[2 * b + hp], zpad], axis=0).T
            halves.append(z[0:SCAN_T] + pltpu.roll(z[SCAN_T:2 * SCAN_T], hw, 1))
        ys = jnp.concatenate(halves, axis=1)
        mean = _head_sums(ys, head_of_lane) * (1.0 / RWKV_HEAD)
        d = ys - mean
        var = _head_sums(d * d, head_of_lane) * (1.0 / RWKV_HEAD)
        yn = d * lax.rsqrt(var + GN_EPS) * lnw_ref[...] + lnb_ref[...]
        yn = yn + bonus_v[b]
        yc_ref[b] = (yn * _silu(gc_ref[b])).astype(BF16)

    @pl.when(tb == nblk - 1)
    def _():
        conv_ref[...] = cu_ref[...]
        shift_ref[...] = cz_ref[...]
        for pr in range(npair):
            b, hp = divmod(pr, 2)
            wkv_ref[b, 2 * hp] = s_ref[pr, :, 0:hw]
            wkv_ref[b, 2 * hp + 1] = s_ref[pr, :, hw:2 * hw]


def _mix(za, zc, gc, conv0, shift0, wkv0, lw, batch, t):
    groups = batch // SCAN_GROUP
    nblk = t // SCAN_T
    npair = SCAN_GROUP * RWKV_HEADS // 2
    seq = lambda w: pl.BlockSpec((SCAN_GROUP, SCAN_T, w), lambda g, i: (g, i, 0))
    state3 = lambda a: pl.BlockSpec((SCAN_GROUP,) + a.shape[1:], lambda g, i: (g, 0, 0))
    state4 = pl.BlockSpec((SCAN_GROUP, RWKV_HEADS, RWKV_HEAD, RWKV_HEAD),
                          lambda g, i: (g, 0, 0, 0))
    full = lambda a: pl.BlockSpec(a.shape, lambda g, i: (0,) * a.ndim)
    r_, c_ = np.arange(2 * LANES)[:, None], np.arange(LANES)[None, :]
    wtab = jnp.asarray(((r_ & (SCAN_T - 1)) == np.arange(SCAN_T)[:, None, None])
                       & (((r_ >> 6) & 1) == (c_ >> 6)), dtype=BF16)
    params = (lw["conv_w"], lw["shift_mu"], lw["decay_w0"], lw["w2a2"], lw["iclr_a0"],
              lw["key_kk"], lw["key_ka"], lw["bonus_rk"], lw["lnx_w"], lw["lnx_b"], wtab)
    out_shape = (
        jax.ShapeDtypeStruct((batch, t, CONV_W), BF16),
        jax.ShapeDtypeStruct((batch, t, RWKV_W), BF16),
        jax.ShapeDtypeStruct(conv0.shape, F32),
        jax.ShapeDtypeStruct(shift0.shape, F32),
        jax.ShapeDtypeStruct(wkv0.shape, F32),
    )
    return pl.pallas_call(
        functools.partial(_mix_kernel, nblk=nblk),
        grid=(groups, nblk),
        in_specs=[seq(4 * CONV_W), seq(SHIFT_W), seq(RWKV_W), state3(conv0), state3(shift0),
                  state4] + [full(p) for p in params],
        out_specs=(seq(CONV_W), seq(RWKV_W), state3(conv0), state3(shift0), state4),
        out_shape=out_shape,
        scratch_shapes=[
            pltpu.VMEM((npair, RWKV_HEAD, LANES), F32),
            pltpu.VMEM((SCAN_GROUP, CONV_K - 1, CONV_W), F32),
            pltpu.VMEM((SCAN_GROUP, 1, SHIFT_W), F32),
            pltpu.VMEM((5, npair, SCAN_T, LANES), F32),
            pltpu.VMEM((npair * RWKV_HEAD, 2 * LANES), BF16),
            pltpu.VMEM((npair, RWKV_HEAD, LANES), F32),
        ],
        compiler_params=pltpu.CompilerParams(dimension_semantics=("parallel", "arbitrary"),
                                             vmem_limit_bytes=VMEM_LIMIT),
        name="mix",
    )(za, zc, gc, conv0, shift0, wkv0, *params)


def _out_kernel(ya_ref, yb_ref, yc_ref, x_ref, wa_ref, wb_ref, wc_ref, fg_ref, o_ref, *, final):
    y = (_dot(ya_ref[...], wa_ref[...]) + _dot(yb_ref[...], wb_ref[...])
         + _dot(yc_ref[...], wc_ref[...]))
    y = x_ref[...] + y
    if final:
        y = _rms(y, fg_ref[...])
    o_ref[...] = y


def _out(ya, yb, yc, x, lw, final_g, final, tm):
    rows = x.shape[0]
    row = lambda w: pl.BlockSpec((tm, w), lambda i: (i, 0))
    full = lambda a: pl.BlockSpec(a.shape, lambda i: (0,) * a.ndim)
    weights = (lw["wo_a"], lw["wo_b"], lw["wo_c"], final_g)
    return pl.pallas_call(
        functools.partial(_out_kernel, final=final),
        grid=(rows // tm,),
        in_specs=[row(CONV_W), row(MLA_W), row(RWKV_W), row(D_MODEL)] + [full(w) for w in weights],
        out_specs=row(D_MODEL),
        out_shape=jax.ShapeDtypeStruct((rows, D_MODEL), F32),
        compiler_params=pltpu.CompilerParams(dimension_semantics=("parallel",),
                                             vmem_limit_bytes=VMEM_LIMIT),
        name="out",
    )(ya, yb, yc, x, *weights)


def _layer_weights(l, norm_g, w_in, conv_w, q_norm_g, w_uq, kv_norm_g, w_ukv, shift_mu,
                   decay_w0, decay_w2, iclr_a0, iclr_a2, key_kk, key_ka, bonus_rk, lnx_w, lnx_b,
                   w_out):
    row = lambda p: p[l].reshape(1, -1).astype(F32)
    wi = w_in[l].astype(BF16)
    o = np.cumsum([0, 4 * CONV_W, Q_LORA, KV_LORA, QK_ROPE, MLA_W, SHIFT_W, RWKV_W]).tolist()
    cols = lambda j: wi[:, o[j]:o[j + 1]]
    uq = w_uq[l].astype(BF16).reshape(Q_LORA, MLA_HEADS, QK_NOPE + QK_ROPE)
    uq_rope = jnp.pad(uq[:, :, QK_NOPE:], ((0, 0), (0, 0), (0, LANES - QK_ROPE)))
    ukv = w_ukv[l].astype(BF16).reshape(KV_LORA, MLA_HEADS, QK_NOPE + V_DIM)
    eye = jnp.eye(MLA_HEADS, dtype=BF16)
    bd = jnp.einsum("chd,hg->hdgc", ukv[:, :, :QK_NOPE], eye).reshape(
        MLA_HEADS * QK_NOPE, MLA_HEADS * KV_LORA)
    wv = jnp.einsum("che,hg->hcge", ukv[:, :, QK_NOPE:], eye).reshape(
        MLA_HEADS, KV_LORA, MLA_W)
    zeros = jnp.zeros((DECAY_LORA, RWKV_W), BF16)
    w2a2 = jnp.concatenate([
        jnp.concatenate([decay_w2[l].astype(BF16), zeros], axis=1),
        jnp.concatenate([zeros, iclr_a2[l].astype(BF16)], axis=1)], axis=0)
    wo = w_out[l].astype(BF16)
    return {
        "norm_g": row(norm_g), "wa": cols(0), "wq": cols(1), "wckv": cols(2),
        "wkr": jnp.pad(cols(3), ((0, 0), (0, LANES - QK_ROPE))),
        "wgb": cols(4), "wzc": cols(5), "wgc": cols(6),
        "q_norm_g": row(q_norm_g),
        "wuq": jnp.concatenate([uq[:, :, :QK_NOPE].reshape(Q_LORA, -1),
                                uq_rope.reshape(Q_LORA, -1)], axis=1),
        "bd": bd, "wv": wv, "kv_norm_g": row(kv_norm_g),
        "conv_w": conv_w[l].astype(F32), "shift_mu": row(shift_mu), "decay_w0": row(decay_w0),
        "w2a2": w2a2, "iclr_a0": row(iclr_a0), "key_kk": row(key_kk), "key_ka": row(key_ka),
        "bonus_rk": row(bonus_rk), "lnx_w": row(lnx_w), "lnx_b": row(lnx_b),
        "wo_a": wo[0:CONV_W], "wo_b": wo[CONV_W:CONV_W + MLA_W], "wo_c": wo[CONV_W + MLA_W:],
    }


def _rope_tables(pos):
    half = QK_ROPE // 2
    inv = ROPE_BASE ** (-jnp.arange(half, dtype=F32) * 2.0 / QK_ROPE)
    ang = pos.astype(F32)[:, None] * inv[None, :]
    cos, sin = jnp.cos(ang), jnp.sin(ang)
    z = jnp.zeros((pos.shape[0], LANES - QK_ROPE), F32)
    zh = jnp.zeros((pos.shape[0], half), F32)
    return (jnp.concatenate([cos, cos, z], axis=1),
            jnp.concatenate([-sin, zh, z], axis=1),
            jnp.concatenate([zh, sin, z], axis=1))


def _stream(x, batch, t, tm, tabs, tab_blocks, states, attn, lws, final_g):
    outs = []
    depth = len(lws)
    for l, lw in enumerate(lws):
        za, qp, kcat, ckv, kr, gb, zc, gc = _proj(x, lw, tabs, tm, tab_blocks)
        yb = attn(l, lw, qp, kcat, gb)
        conv0, shift0, wkv0 = states[l]
        ya, yc, conv, shift, wkv = _mix(
            za.reshape(batch, t, -1), zc.reshape(batch, t, -1), gc.reshape(batch, t, -1),
            conv0, shift0, wkv0, lw, batch, t)
        x = _out(ya.reshape(batch * t, -1), yb, yc.reshape(batch * t, -1), x, lw, final_g,
                 l == depth - 1, tm)
        outs.append((ckv.reshape(batch, t, -1), kr.reshape(batch, t, -1), conv,
                     shift.reshape(batch, -1), wkv))
    return x, outs


def kernel(x_prompt, x_sample, cache_ckv, cache_krope, state_conv, state_shift, state_wkv,
           meta_tokens, norm_g, w_in, conv_w, q_norm_g, w_uq, kv_norm_g, w_ukv, shift_mu,
           decay_w0, decay_w2, iclr_a0, iclr_a2, key_kk, key_ka, bonus_rk, lnx_w, lnx_b,
           w_out, final_g):
    depth = w_in.shape[0]
    bp, seq = x_prompt.shape[:2]
    bs, ts = x_sample.shape[:2]
    past = cache_ckv.shape[2]
    tp = FRONT + N_META + seq
    assert tp % KTILE == 0 and bp % SCAN_GROUP == 0 and bs % SCAN_GROUP == 0
    assert seq % CHUNK == 0 and ts == SCAN_T and tp % SCAN_T == 0
    lws = [_layer_weights(l, norm_g, w_in, conv_w, q_norm_g, w_uq, kv_norm_g, w_ukv, shift_mu,
                          decay_w0, decay_w2, iclr_a0, iclr_a2, key_kk, key_ka, bonus_rk,
                          lnx_w, lnx_b, w_out) for l in range(depth)]
    fg = final_g.reshape(1, -1).astype(F32)

    xp = jnp.concatenate([
        jnp.zeros((bp, FRONT, D_MODEL), F32),
        jnp.broadcast_to(meta_tokens.astype(F32)[None], (bp, N_META, D_MODEL)),
        x_prompt], axis=1).reshape(bp * tp, D_MODEL)
    tm_p = tp // PROMPT_ROW_TILES
    tabs_p = _rope_tables(jnp.maximum(jnp.arange(tp, dtype=jnp.int32) - FRONT, 0))
    zero_states = [(jnp.zeros((bp, CONV_K - 1, CONV_W), F32), jnp.zeros((bp, 1, SHIFT_W), F32),
                    jnp.zeros((bp, RWKV_HEADS, RWKV_HEAD, RWKV_HEAD), F32))] * depth
    attn_p = lambda l, lw, qp, kcat, gb: _attn_prompt(qp, kcat, gb, lw["wv"], bp, tp)
    yp, outs_p = _stream(xp, bp, tp, tm_p, tabs_p, tp // tm_p, zero_states, attn_p, lws, fg)

    tm_s = 512
    reps = tm_s // ts
    tabs_s = tuple(jnp.tile(tb_, (reps, 1))
                   for tb_ in _rope_tables(past + jnp.arange(ts, dtype=jnp.int32)))
    states_s = [(state_conv[l], state_shift[l].reshape(bs, 1, SHIFT_W), state_wkv[l])
                for l in range(depth)]
    attn_s = lambda l, lw, qp, kcat, gb: _attn_sample(qp, kcat, cache_ckv, cache_krope, l, gb,
                                                      lw["wv"], bs, ts)
    ys, outs_s = _stream(x_sample.reshape(bs * ts, D_MODEL), bs, ts, tm_s, tabs_s, 1,
                         states_s, attn_s, lws, fg)

    stack = lambda outs, j: jnp.stack([o[j] for o in outs])
    y_prompt = yp.reshape(bp, tp, D_MODEL)[:, FRONT + N_META:]
    ckv_p = stack(outs_p, 0)[:, :, FRONT:]
    kr_p = stack(outs_p, 1)[:, :, FRONT:]
    return (y_prompt, ys.reshape(bs, ts, D_MODEL), ckv_p, kr_p, stack(outs_p, 2),
            stack(outs_p, 3), stack(outs_p, 4),
            stack(outs_s, 0), stack(outs_s, 1), stack(outs_s, 2), stack(outs_s, 3),
            stack(outs_s, 4))
```

```python
import functools

import jax
import jax.numpy as jnp
import numpy as np
from jax import lax
from jax.experimental import pallas as pl
from jax.experimental.pallas import tpu as pltpu

D_MODEL = 1024
CHUNK = 64
N_META = 16
CONV_W = 256
CONV_K = 3
MLA_HEADS = 8
QK_NOPE = 64
QK_ROPE = 32
V_DIM = 64
MLA_W = MLA_HEADS * V_DIM
Q_LORA = 256
KV_LORA = 128
RWKV_HEAD = 64
RWKV_W = 256
RWKV_HEADS = 4
DECAY_LORA = 64
ICLR_LORA = 64
SHIFT_W = 3 * RWKV_W + DECAY_LORA + ICLR_LORA
ROPE_BASE = 10000.0
RMS_EPS = 1e-6
GN_EPS = 64e-5
NEG = -1e30
LOG2_E = 1.4426950408889634
DECAY_SCALE = 0.6065306597126334

LANES = 128
QTILE = 128
KTILE = 256
FRONT = KTILE - N_META
QK_W = 2 * LANES
PROMPT_ROW_TILES = 8
SCAN_GROUP = 4
SCAN_T = 64
SCAN_SPLIT = 2
SCAN_UNROLL = 8
VMEM_LIMIT = 48 * 1024 * 1024

F32 = jnp.float32
BF16 = jnp.bfloat16


def _dot(a, b):
    return jnp.dot(a, b, preferred_element_type=F32)


def _dot_nt(a, b):
    return lax.dot_general(a, b, (((1,), (1,)), ((), ())), preferred_element_type=F32)


def _rms(x, g):
    return x * lax.rsqrt(jnp.mean(x * x, axis=-1, keepdims=True) + RMS_EPS) * g


def _silu(x):
    return x * jax.nn.sigmoid(x)


def _rope_block(x, cos, sin_lo, sin_hi):
    return (x * cos + pltpu.roll(x, LANES - QK_ROPE // 2, 1) * sin_lo
            + pltpu.roll(x, QK_ROPE // 2, 1) * sin_hi)


def _proj_kernel(x_ref, g_ref, wa_ref, wq_ref, wckv_ref, wkr_ref, wgb_ref, wzc_ref, wgc_ref,
                 qg_ref, wuq_ref, bd_ref, kvg_ref, cos_ref, slo_ref, shi_ref,
                 za_ref, qp_ref, kcat_ref, ckv_ref, kr_ref, gb_ref, zc_ref, gc_ref):
    scale = (QK_NOPE + QK_ROPE) ** -0.5 * LOG2_E
    hb = _rms(x_ref[...], g_ref[...]).astype(BF16)
    za_ref[...] = _dot(hb, wa_ref[...])
    gb_ref[...] = _dot(hb, wgb_ref[...])
    zc_ref[...] = _dot(hb, wzc_ref[...])
    gc_ref[...] = _dot(hb, wgc_ref[...])

    cos, slo, shi = cos_ref[...], slo_ref[...], shi_ref[...]
    cqn = _rms(_dot(hb, wq_ref[...]), qg_ref[...]).astype(BF16)
    qe = _dot(cqn, wuq_ref[...])
    nope_w = MLA_HEADS * QK_NOPE
    qabs = _dot(qe[:, :nope_w].astype(BF16), bd_ref[...])
    for h in range(MLA_HEADS):
        lo, hi = h * LANES, (h + 1) * LANES
        rot = _rope_block(qe[:, nope_w + lo:nope_w + hi], cos, slo, shi)
        qp_ref[h, :, 0:LANES] = (qabs[:, lo:hi] * scale).astype(BF16)
        qp_ref[h, :, LANES:QK_W] = (rot * scale).astype(BF16)

    ckvn = _rms(_dot(hb, wckv_ref[...]), kvg_ref[...])
    ckv_ref[...] = ckvn
    krr = _rope_block(_dot(hb, wkr_ref[...]), cos, slo, shi)
    kr_ref[...] = krr[:, :QK_ROPE]
    kcat_ref[:, 0:LANES] = ckvn.astype(BF16)
    kcat_ref[:, LANES:QK_W] = krr.astype(BF16)


def _proj(x, lw, tabs, tm, tab_blocks):
    rows = x.shape[0]
    n = rows // tm
    row = lambda w: pl.BlockSpec((tm, w), lambda i: (i, 0))
    full = lambda a: pl.BlockSpec(a.shape, lambda i: (0,) * a.ndim)
    tab = pl.BlockSpec((tm, LANES), lambda i: (i % tab_blocks, 0))
    weights = (lw["norm_g"], lw["wa"], lw["wq"], lw["wckv"], lw["wkr"], lw["wgb"], lw["wzc"],
               lw["wgc"], lw["q_norm_g"], lw["wuq"], lw["bd"], lw["kv_norm_g"])
    out_shape = (
        jax.ShapeDtypeStruct((rows, 4 * CONV_W), F32),
        jax.ShapeDtypeStruct((MLA_HEADS, rows, QK_W), BF16),
        jax.ShapeDtypeStruct((rows, QK_W), BF16),
        jax.ShapeDtypeStruct((rows, KV_LORA), F32),
        jax.ShapeDtypeStruct((rows, QK_ROPE), F32),
        jax.ShapeDtypeStruct((rows, MLA_W), F32),
        jax.ShapeDtypeStruct((rows, SHIFT_W), F32),
        jax.ShapeDtypeStruct((rows, RWKV_W), F32),
    )
    out_specs = (row(4 * CONV_W), pl.BlockSpec((MLA_HEADS, tm, QK_W), lambda i: (0, i, 0)),
                 row(QK_W), row(KV_LORA), row(QK_ROPE), row(MLA_W), row(SHIFT_W), row(RWKV_W))
    return pl.pallas_call(
        _proj_kernel,
        grid=(n,),
        in_specs=[row(D_MODEL)] + [full(w) for w in weights] + [tab, tab, tab],
        out_specs=out_specs,
        out_shape=out_shape,
        compiler_params=pltpu.CompilerParams(dimension_semantics=("parallel",),
                                             vmem_limit_bytes=VMEM_LIMIT),
        name="proj",
    )(x, *weights, *tabs)


def _heads_out(o_lat, wv_ref, rows):
    out = None
    for h in range(MLA_HEADS):
        part = _dot(o_lat[h * rows:(h + 1) * rows].astype(BF16), wv_ref[h])
        out = part if out is None else out + part
    return out


def _attn_prompt_kernel(q_ref, k_ref, gb_ref, wv_ref, yb_ref, s_ref, mx_ref, ls_ref, acc_ref):
    i = pl.program_id(1)
    rows = MLA_HEADS * QTILE

    @pl.when(i == 0)
    def _():
        yb_ref[...] = jnp.zeros(yb_ref.shape, BF16)

    @pl.when(i > 0)
    def _():
        q = q_ref[...].reshape(rows, QK_W)
        col = lax.broadcasted_iota(jnp.int32, (rows, KTILE), 1)
        qrow = (lax.broadcasted_iota(jnp.int32, (rows, KTILE), 0) & (QTILE - 1)) + i * QTILE
        last = i >> 1

        def keys(j, width):
            return k_ref[0, pl.ds(pl.multiple_of(j * KTILE, KTILE), width), :]

        def s_at(j, width):
            return s_ref.at[:, pl.ds(pl.multiple_of(j * KTILE, KTILE), width)]

        def fold(x, op):
            out = x[:, :LANES]
            for c in range(1, x.shape[1] // LANES):
                out = op(out, x[:, c * LANES:(c + 1) * LANES])
            return out

        def scores(j, width, mask=None, first=False):
            s = _dot_nt(q, keys(j, width))
            if mask is not None:
                s = jnp.where(mask, s, NEG)
            s_at(j, width)[...] = s
            sm = fold(s, jnp.maximum)
            mx_ref[...] = sm if first else jnp.maximum(mx_ref[...], sm)

        scores(0, KTILE, col >= FRONT, first=True)
        inner = jnp.maximum(last - 1, 0)

        def wide1(jj, carry):
            scores(1 + 2 * jj, 2 * KTILE)
            return carry

        lax.fori_loop(0, inner >> 1, wide1, 0)

        @pl.when((inner & 1) == 1)
        def _():
            scores(last - 1, KTILE)

        @pl.when(last > 0)
        def _():
            scores(last, KTILE, ((col + last * KTILE) >> 6) <= (qrow >> 6))

        m = jnp.broadcast_to(jnp.max(mx_ref[...], axis=-1, keepdims=True), (rows, LANES))
        mx_ref[...] = m
        ls_ref[...] = jnp.zeros((rows, LANES), F32)
        acc_ref[...] = jnp.zeros((rows, KV_LORA), F32)

        def weigh(j, width):
            s = s_at(j, width)[...]
            m_ = mx_ref[...]
            p = [jnp.exp2(s[:, c * LANES:(c + 1) * LANES] - m_) for c in range(width // LANES)]
            ls_ref[...] += functools.reduce(lambda a, b: a + b, p)
            pb = jnp.concatenate([x.astype(BF16) for x in p], axis=1)
            acc_ref[...] += _dot(pb, keys(j, width)[:, :KV_LORA])

        def wide2(jj, carry):
            weigh(2 * jj, 2 * KTILE)
            return carry

        ntile = last + 1
        lax.fori_loop(0, ntile >> 1, wide2, 0)

        @pl.when((ntile & 1) == 1)
        def _():
            weigh(last, KTILE)

        o_lat = acc_ref[...] / jnp.sum(ls_ref[...], axis=-1, keepdims=True)
        yb_ref[...] = (_heads_out(o_lat, wv_ref, QTILE) * _silu(gb_ref[...])).astype(BF16)


def _attn_prompt(qp, kcat, gb, wv, batch, tp):
    nq = tp // QTILE
    rows = MLA_HEADS * QTILE
    return pl.pallas_call(
        _attn_prompt_kernel,
        grid=(batch, nq),
        in_specs=[
            pl.BlockSpec((MLA_HEADS, QTILE, QK_W), lambda b, i: (0, b * nq + i, 0)),
            pl.BlockSpec((1, tp, QK_W), lambda b, i: (b, 0, 0)),
            pl.BlockSpec((QTILE, MLA_W), lambda b, i: (b * nq + i, 0)),
            pl.BlockSpec(wv.shape, lambda b, i: (0, 0, 0)),
        ],
        out_specs=pl.BlockSpec((QTILE, MLA_W), lambda b, i: (b * nq + i, 0)),
        out_shape=jax.ShapeDtypeStruct((batch * tp, MLA_W), BF16),
        scratch_shapes=[pltpu.VMEM((rows, tp), F32),
                        pltpu.VMEM((rows, LANES), F32),
                        pltpu.VMEM((rows, LANES), F32),
                        pltpu.VMEM((rows, KV_LORA), F32)],
        compiler_params=pltpu.CompilerParams(dimension_semantics=("parallel", "arbitrary"),
                                             vmem_limit_bytes=VMEM_LIMIT),
        name="attn_prompt",
    )(qp, kcat.reshape(batch, tp, QK_W), gb, wv)


def _attn_sample_kernel(q_ref, kn_ref, ck_ref, kr_ref, gb_ref, wv_ref, yb_ref):
    ts = q_ref.shape[1]
    q = q_ref[...].reshape(MLA_HEADS * ts, QK_W)
    kn = kn_ref[...]
    ck = ck_ref[0, 0].astype(BF16)
    kr = kr_ref[0, 0].astype(BF16)
    s_past = _dot_nt(q[:, :KV_LORA], ck) + _dot_nt(q[:, KV_LORA:KV_LORA + QK_ROPE], kr)
    s_new = _dot_nt(q, kn)
    m = jnp.maximum(jnp.max(s_past, axis=-1, keepdims=True),
                    jnp.max(s_new, axis=-1, keepdims=True))
    p_past = jnp.exp2(s_past - m)
    p_new = jnp.exp2(s_new - m)
    l = jnp.sum(p_past, axis=-1, keepdims=True) + jnp.sum(p_new, axis=-1, keepdims=True)
    o_lat = (_dot(p_past.astype(BF16), ck) + _dot(p_new.astype(BF16), kn[:, :KV_LORA])) / l
    yb_ref[...] = (_heads_out(o_lat, wv_ref, ts) * _silu(gb_ref[...])).astype(BF16)


def _attn_sample(qp, kcat, cache_ckv, cache_krope, layer, gb, wv, batch, ts):
    past = cache_ckv.shape[2]
    return pl.pallas_call(
        _attn_sample_kernel,
        grid=(batch,),
        in_specs=[
            pl.BlockSpec((MLA_HEADS, ts, QK_W), lambda b: (0, b, 0)),
            pl.BlockSpec((ts, QK_W), lambda b: (b, 0)),
            pl.BlockSpec((1, 1, past, KV_LORA), lambda b: (layer, b, 0, 0)),
            pl.BlockSpec((1, 1, past, QK_ROPE), lambda b: (layer, b, 0, 0)),
            pl.BlockSpec((ts, MLA_W), lambda b: (b, 0)),
            pl.BlockSpec(wv.shape, lambda b: (0, 0, 0)),
        ],
        out_specs=pl.BlockSpec((ts, MLA_W), lambda b: (b, 0)),
        out_shape=jax.ShapeDtypeStruct((batch * ts, MLA_W), BF16),
        compiler_params=pltpu.CompilerParams(dimension_semantics=("parallel",),
                                             vmem_limit_bytes=VMEM_LIMIT),
        name="attn_sample",
    )(qp, kcat, cache_ckv, cache_krope, gb, wv)


def _head_sums(x, ones4):
    hi = x.astype(BF16)
    lo = (x - hi.astype(F32)).astype(BF16)
    return _dot(hi, ones4) + _dot(lo, ones4)


def _mix_kernel(za_ref, zc_ref, gc_ref, conv0_ref, shift0_ref, wkv0_ref,
                cw_ref, mu_ref, w0_ref, w2a2_ref, a0_ref, kkw_ref, kaw_ref, bonus_ref,
                lnw_ref, lnb_ref, wtab_ref,
                ya_ref, yc_ref, conv_ref, shift_ref, wkv_ref,
                s_ref, cu_ref, cz_ref, vec_ref, vtl_ref, yt_ref, *, nblk):
    tb = pl.program_id(1)
    tblk = SCAN_T
    npair = SCAN_GROUP * RWKV_HEADS // 2
    hw = RWKV_HEAD

    @pl.when(tb == 0)
    def _():
        for pr in range(npair):
            b, hp = divmod(pr, 2)
            s_ref[pr, :, 0:hw] = wkv0_ref[b, 2 * hp]
            s_ref[pr, :, hw:2 * hw] = wkv0_ref[b, 2 * hp + 1]
        cu_ref[...] = conv0_ref[...]
        cz_ref[...] = shift0_ref[...]

    trow = lax.broadcasted_iota(jnp.int32, (tblk, 1), 0)
    lane128 = lax.broadcasted_iota(jnp.int32, (1, LANES), 1)
    ri = lax.broadcasted_iota(jnp.int32, (2 * LANES, 2 * LANES), 0)
    ci = lax.broadcasted_iota(jnp.int32, (2 * LANES, 2 * LANES), 1)
    ones4 = jnp.where((ri >> 6) == (ci >> 6), 1.0, 0.0).astype(BF16)
    cw = cw_ref[...]
    zpad = jnp.zeros((LANES - hw, LANES), F32)
    bonus_v = []
    for b in range(SCAN_GROUP):
        za = za_ref[b]
        xin, bg = za[:, 0:CONV_W], za[:, CONV_W:2 * CONV_W]
        cg, ga = za[:, 2 * CONV_W:3 * CONV_W], za[:, 3 * CONV_W:4 * CONV_W]
        u = cg * xin
        cu = cu_ref[b]
        u1 = jnp.where(trow == 0, cu[1:2], pltpu.roll(u, 1, 0))
        u2 = jnp.where(trow == 0, cu[0:1], jnp.where(trow == 1, cu[1:2], pltpu.roll(u, 2, 0)))
        conv = cw[0:1] * u2 + cw[1:2] * u1 + cw[2:3] * u
        ya_ref[b] = (bg * conv * _silu(ga)).astype(BF16)
        cu_ref[b] = u[tblk - (CONV_K - 1):tblk]

        zc = zc_ref[b]
        prev = jnp.where(trow == 0, cz_ref[b], pltpu.roll(zc, 1, 0))
        cz_ref[b] = zc[tblk - 1:tblk]
        zs = zc + (prev - zc) * mu_ref[...]
        r, k, v = zs[:, 0:RWKV_W], zs[:, RWKV_W:2 * RWKV_W], zs[:, 2 * RWKV_W:3 * RWKV_W]
        la = zs[:, 3 * RWKV_W:SHIFT_W]
        la = jnp.where(lane128 < DECAY_LORA, jnp.tanh(la), la)
        lora = _dot(la.astype(BF16), w2a2_ref[...])
        dec_in = w0_ref[...] + lora[:, 0:RWKV_W]
        decay = jnp.exp(-DECAY_SCALE * jax.nn.sigmoid(dec_in))
        a = jax.nn.sigmoid(a0_ref[...] + lora[:, RWKV_W:2 * RWKV_W])
        kk = k * kkw_ref[...]
        kk = kk / jnp.maximum(jnp.sqrt(_head_sums(kk * kk, ones4)), 1e-12)
        kp = k * (1.0 + (a - 1.0) * kaw_ref[...])
        bonus_v.append(_head_sums(r * kp * bonus_ref[...], ones4) * v)

        for n, vec in enumerate((-kk, decay, kk * a, kp, r)):
            vec_ref[n, 2 * b] = vec[:, 0:LANES]
            vec_ref[n, 2 * b + 1] = vec[:, LANES:2 * LANES]
        for hp in range(2):
            z = jnp.concatenate([v[:, hp * LANES:(hp + 1) * LANES], zpad], axis=0).T
            vt = z[0:hw] + pltpu.roll(z[hw:2 * hw], SCAN_T, 1)
            side, n = divmod(2 * b + hp, npair // 2)
            vtl_ref[n * hw:(n + 1) * hw, side * LANES:(side + 1) * LANES] = vt.astype(BF16)

    tok_of_lane = lane128 & (SCAN_T - 1)
    gsz = npair // SCAN_SPLIT
    groups = [list(range(g * gsz, (g + 1) * gsz)) for g in range(SCAN_SPLIT)]

    def lane_sums(s_list, prs, t_a, t_b):
        lhs = jnp.concatenate(
            [jnp.concatenate([(s * vec_ref[0, pr, pl.ds(t_a, 1), :]).astype(BF16),
                              (s * vec_ref[4, pr, pl.ds(t_b, 1), :]).astype(BF16)], axis=1)
             for s, pr in zip(s_list, prs)], axis=0)
        return _dot(lhs, ones4)

    def step(t, carry):
        rowv = lambda n, pr: vec_ref[n, pr, pl.ds(t, 1), :]
        vb = _dot(vtl_ref[...], wtab_ref[t])
        hit = tok_of_lane == t - 1
        for prs in groups:
            s_old = [s_ref[pr] for pr in prs]
            sums = lane_sums(s_old, prs, t, jnp.maximum(t - 1, 0))
            for n, pr in enumerate(prs):
                rows = slice(n * hw, (n + 1) * hw)
                side, m = divmod(pr, npair // 2)
                vbp = vb[m * hw:(m + 1) * hw, side * LANES:(side + 1) * LANES]
                s_ref[pr] = (s_old[n] * rowv(1, pr) + sums[rows, 0:LANES] * rowv(2, pr)
                             + vbp * rowv(3, pr))
                yt_ref[pr] = jnp.where(hit, sums[rows, LANES:2 * LANES], yt_ref[pr])
        return carry

    yt_ref[...] = jnp.zeros(yt_ref.shape, F32)
    lax.fori_loop(0, SCAN_T, step, 0, unroll=SCAN_UNROLL)
    hit = tok_of_lane == SCAN_T - 1
    for prs in groups:
        sums = lane_sums([s_ref[pr] for pr in prs], prs, SCAN_T - 1, SCAN_T - 1)
        for n, pr in enumerate(prs):
            yt_ref[pr] = jnp.where(hit, sums[n * hw:(n + 1) * hw, LANES:2 * LANES], yt_ref[pr])

    for b in range(SCAN_GROUP):
        halves = []
        for hp in range(2):
            z = jnp.concatenate([yt_ref[2 * b + hp], zpad], axis=0).T
            halves.append(z[0:SCAN_T] + pltpu.roll(z[SCAN_T:2 * SCAN_T], hw, 1))
        ys = jnp.concatenate(halves, axis=1)
        mean = _head_sums(ys, ones4) * (1.0 / RWKV_HEAD)
        d = ys - mean
        var = _head_sums(d * d, ones4) * (1.0 / RWKV_HEAD)
        yn = d * lax.rsqrt(var + GN_EPS) * lnw_ref[...] + lnb_ref[...]
        yn = yn + bonus_v[b]
        yc_ref[b] = (yn * _silu(gc_ref[b])).astype(BF16)

    @pl.when(tb == nblk - 1)
    def _():
        conv_ref[...] = cu_ref[...]
        shift_ref[...] = cz_ref[...]
        for pr in range(npair):
            b, hp = divmod(pr, 2)
            wkv_ref[b, 2 * hp] = s_ref[pr, :, 0:hw]
            wkv_ref[b, 2 * hp + 1] = s_ref[pr, :, hw:2 * hw]


def _mix(za, zc, gc, conv0, shift0, wkv0, lw, batch, t):
    groups = batch // SCAN_GROUP
    nblk = t // SCAN_T
    npair = SCAN_GROUP * RWKV_HEADS // 2
    seq = lambda w: pl.BlockSpec((SCAN_GROUP, SCAN_T, w), lambda g, i: (g, i, 0))
    state3 = lambda a: pl.BlockSpec((SCAN_GROUP,) + a.shape[1:], lambda g, i: (g, 0, 0))
    state4 = pl.BlockSpec((SCAN_GROUP, RWKV_HEADS, RWKV_HEAD, RWKV_HEAD),
                          lambda g, i: (g, 0, 0, 0))
    full = lambda a: pl.BlockSpec(a.shape, lambda g, i: (0,) * a.ndim)
    r_, c_ = np.arange(2 * LANES)[:, None], np.arange(2 * LANES)[None, :]
    wtab = jnp.asarray(((r_ & (SCAN_T - 1)) == np.arange(SCAN_T)[:, None, None])
                       & ((r_ >> 6) == (c_ >> 6)), dtype=BF16)
    params = (lw["conv_w"], lw["shift_mu"], lw["decay_w0"], lw["w2a2"], lw["iclr_a0"],
              lw["key_kk"], lw["key_ka"], lw["bonus_rk"], lw["lnx_w"], lw["lnx_b"], wtab)
    out_shape = (
        jax.ShapeDtypeStruct((batch, t, CONV_W), BF16),
        jax.ShapeDtypeStruct((batch, t, RWKV_W), BF16),
        jax.ShapeDtypeStruct(conv0.shape, F32),
        jax.ShapeDtypeStruct(shift0.shape, F32),
        jax.ShapeDtypeStruct(wkv0.shape, F32),
    )
    return pl.pallas_call(
        functools.partial(_mix_kernel, nblk=nblk),
        grid=(groups, nblk),
        in_specs=[seq(4 * CONV_W), seq(SHIFT_W), seq(RWKV_W), state3(conv0), state3(shift0),
                  state4] + [full(p) for p in params],
        out_specs=(seq(CONV_W), seq(RWKV_W), state3(conv0), state3(shift0), state4),
        out_shape=out_shape,
        scratch_shapes=[
            pltpu.VMEM((npair, RWKV_HEAD, LANES), F32),
            pltpu.VMEM((SCAN_GROUP, CONV_K - 1, CONV_W), F32),
            pltpu.VMEM((SCAN_GROUP, 1, SHIFT_W), F32),
            pltpu.VMEM((5, npair, SCAN_T, LANES), F32),
            pltpu.VMEM((npair // 2 * RWKV_HEAD, 2 * LANES), BF16),
            pltpu.VMEM((npair, RWKV_HEAD, LANES), F32),
        ],
        compiler_params=pltpu.CompilerParams(dimension_semantics=("parallel", "arbitrary"),
                                             vmem_limit_bytes=VMEM_LIMIT),
        name="mix",
    )(za, zc, gc, conv0, shift0, wkv0, *params)


def _out_kernel(ya_ref, yb_ref, yc_ref, x_ref, wa_ref, wb_ref, wc_ref, fg_ref, o_ref, *, final):
    y = (_dot(ya_ref[...], wa_ref[...]) + _dot(yb_ref[...], wb_ref[...])
         + _dot(yc_ref[...], wc_ref[...]))
    y = x_ref[...] + y
    if final:
        y = _rms(y, fg_ref[...])
    o_ref[...] = y


def _out(ya, yb, yc, x, lw, final_g, final, tm):
    rows = x.shape[0]
    row = lambda w: pl.BlockSpec((tm, w), lambda i: (i, 0))
    full = lambda a: pl.BlockSpec(a.shape, lambda i: (0,) * a.ndim)
    weights = (lw["wo_a"], lw["wo_b"], lw["wo_c"], final_g)
    return pl.pallas_call(
        functools.partial(_out_kernel, final=final),
        grid=(rows // tm,),
        in_specs=[row(CONV_W), row(MLA_W), row(RWKV_W), row(D_MODEL)] + [full(w) for w in weights],
        out_specs=row(D_MODEL),
        out_shape=jax.ShapeDtypeStruct((rows, D_MODEL), F32),
        compiler_params=pltpu.CompilerParams(dimension_semantics=("parallel",),
                                             vmem_limit_bytes=VMEM_LIMIT),
        name="out",
    )(ya, yb, yc, x, *weights)


def _layer_weights(l, norm_g, w_in, conv_w, q_norm_g, w_uq, kv_norm_g, w_ukv, shift_mu,
                   decay_w0, decay_w2, iclr_a0, iclr_a2, key_kk, key_ka, bonus_rk, lnx_w, lnx_b,
                   w_out):
    row = lambda p: p[l].reshape(1, -1).astype(F32)
    wi = w_in[l].astype(BF16)
    o = np.cumsum([0, 4 * CONV_W, Q_LORA, KV_LORA, QK_ROPE, MLA_W, SHIFT_W, RWKV_W]).tolist()
    cols = lambda j: wi[:, o[j]:o[j + 1]]
    uq = w_uq[l].astype(BF16).reshape(Q_LORA, MLA_HEADS, QK_NOPE + QK_ROPE)
    uq_rope = jnp.pad(uq[:, :, QK_NOPE:], ((0, 0), (0, 0), (0, LANES - QK_ROPE)))
    ukv = w_ukv[l].astype(BF16).reshape(KV_LORA, MLA_HEADS, QK_NOPE + V_DIM)
    eye = jnp.eye(MLA_HEADS, dtype=BF16)
    bd = jnp.einsum("chd,hg->hdgc", ukv[:, :, :QK_NOPE], eye).reshape(
        MLA_HEADS * QK_NOPE, MLA_HEADS * KV_LORA)
    wv = jnp.einsum("che,hg->hcge", ukv[:, :, QK_NOPE:], eye).reshape(
        MLA_HEADS, KV_LORA, MLA_W)
    zeros = jnp.zeros((DECAY_LORA, RWKV_W), BF16)
    w2a2 = jnp.concatenate([
        jnp.concatenate([decay_w2[l].astype(BF16), zeros], axis=1),
        jnp.concatenate([zeros, iclr_a2[l].astype(BF16)], axis=1)], axis=0)
    wo = w_out[l].astype(BF16)
    return {
        "norm_g": row(norm_g), "wa": cols(0), "wq": cols(1), "wckv": cols(2),
        "wkr": jnp.pad(cols(3), ((0, 0), (0, LANES - QK_ROPE))),
        "wgb": cols(4), "wzc": cols(5), "wgc": cols(6),
        "q_norm_g": row(q_norm_g),
        "wuq": jnp.concatenate([uq[:, :, :QK_NOPE].reshape(Q_LORA, -1),
                                uq_rope.reshape(Q_LORA, -1)], axis=1),
        "bd": bd, "wv": wv, "kv_norm_g": row(kv_norm_g),
        "conv_w": conv_w[l].astype(F32), "shift_mu": row(shift_mu), "decay_w0": row(decay_w0),
        "w2a2": w2a2, "iclr_a0": row(iclr_a0), "key_kk": row(key_kk), "key_ka": row(key_ka),
        "bonus_rk": row(bonus_rk), "lnx_w": row(lnx_w), "lnx_b": row(lnx_b),
        "wo_a": wo[0:CONV_W], "wo_b": wo[CONV_W:CONV_W + MLA_W], "wo_c": wo[CONV_W + MLA_W:],
    }


def _rope_tables(pos):
    half = QK_ROPE // 2
    inv = ROPE_BASE ** (-jnp.arange(half, dtype=F32) * 2.0 / QK_ROPE)
    ang = pos.astype(F32)[:, None] * inv[None, :]
    cos, sin = jnp.cos(ang), jnp.sin(ang)
    z = jnp.zeros((pos.shape[0], LANES - QK_ROPE), F32)
    zh = jnp.zeros((pos.shape[0], half), F32)
    return (jnp.concatenate([cos, cos, z], axis=1),
            jnp.concatenate([-sin, zh, z], axis=1),
            jnp.concatenate([zh, sin, z], axis=1))


def _stream(x, batch, t, tm, tabs, tab_blocks, states, attn, lws, final_g):
    outs = []
    depth = len(lws)
    for l, lw in enumerate(lws):
        za, qp, kcat, ckv, kr, gb, zc, gc = _proj(x, lw, tabs, tm, tab_blocks)
        yb = attn(l, lw, qp, kcat, gb)
        conv0, shift0, wkv0 = states[l]
        ya, yc, conv, shift, wkv = _mix(
            za.reshape(batch, t, -1), zc.reshape(batch, t, -1), gc.reshape(batch, t, -1),
            conv0, shift0, wkv0, lw, batch, t)
        x = _out(ya.reshape(batch * t, -1), yb, yc.reshape(batch * t, -1), x, lw, final_g,
                 l == depth - 1, tm)
        outs.append((ckv.reshape(batch, t, -1), kr.reshape(batch, t, -1), conv,
                     shift.reshape(batch, -1), wkv))
    return x, outs


def kernel(x_prompt, x_sample, cache_ckv, cache_krope, state_conv, state_shift, state_wkv,
           meta_tokens, norm_g, w_in, conv_w, q_norm_g, w_uq, kv_norm_g, w_ukv, shift_mu,
           decay_w0, decay_w2, iclr_a0, iclr_a2, key_kk, key_ka, bonus_rk, lnx_w, lnx_b,
           w_out, final_g):
    depth = w_in.shape[0]
    bp, seq = x_prompt.shape[:2]
    bs, ts = x_sample.shape[:2]
    past = cache_ckv.shape[2]
    tp = FRONT + N_META + seq
    assert tp % KTILE == 0 and bp % SCAN_GROUP == 0 and bs % SCAN_GROUP == 0
    assert seq % CHUNK == 0 and ts == SCAN_T and tp % SCAN_T == 0
    lws = [_layer_weights(l, norm_g, w_in, conv_w, q_norm_g, w_uq, kv_norm_g, w_ukv, shift_mu,
                          decay_w0, decay_w2, iclr_a0, iclr_a2, key_kk, key_ka, bonus_rk,
                          lnx_w, lnx_b, w_out) for l in range(depth)]
    fg = final_g.reshape(1, -1).astype(F32)

    xp = jnp.concatenate([
        jnp.zeros((bp, FRONT, D_MODEL), F32),
        jnp.broadcast_to(meta_tokens.astype(F32)[None], (bp, N_META, D_MODEL)),
        x_prompt], axis=1).reshape(bp * tp, D_MODEL)
    tm_p = tp // PROMPT_ROW_TILES
    tabs_p = _rope_tables(jnp.maximum(jnp.arange(tp, dtype=jnp.int32) - FRONT, 0))
    zero_states = [(jnp.zeros((bp, CONV_K - 1, CONV_W), F32), jnp.zeros((bp, 1, SHIFT_W), F32),
                    jnp.zeros((bp, RWKV_HEADS, RWKV_HEAD, RWKV_HEAD), F32))] * depth
    attn_p = lambda l, lw, qp, kcat, gb: _attn_prompt(qp, kcat, gb, lw["wv"], bp, tp)
    yp, outs_p = _stream(xp, bp, tp, tm_p, tabs_p, tp // tm_p, zero_states, attn_p, lws, fg)

    tm_s = 512
    reps = tm_s // ts
    tabs_s = tuple(jnp.tile(tb_, (reps, 1))
                   for tb_ in _rope_tables(past + jnp.arange(ts, dtype=jnp.int32)))
    states_s = [(state_conv[l], state_shift[l].reshape(bs, 1, SHIFT_W), state_wkv[l])
                for l in range(depth)]
    attn_s = lambda l, lw, qp, kcat, gb: _attn_sample(qp, kcat, cache_ckv, cache_krope, l, gb,
                                                      lw["wv"], bs, ts)
    ys, outs_s = _stream(x_sample.reshape(bs * ts, D_MODEL), bs, ts, tm_s, tabs_s, 1,
                         states_s, attn_s, lws, fg)

    stack = lambda outs, j: jnp.stack([o[j] for o in outs])
    y_prompt = yp.reshape(bp, tp, D_MODEL)[:, FRONT + N_META:]
    ckv_p = stack(outs_p, 0)[:, :, FRONT:]
    kr_p = stack(outs_p, 1)[:, :, FRONT:]
    return (y_prompt, ys.reshape(bs, ts, D_MODEL), ckv_p, kr_p, stack(outs_p, 2),
            stack(outs_p, 3), stack(outs_p, 4),
            stack(outs_s, 0), stack(outs_s, 1), stack(outs_s, 2), stack(outs_s, 3),
            stack(outs_s, 4))
```

```python
import functools

import jax
import jax.numpy as jnp
import numpy as np
from jax import lax
from jax.experimental import pallas as pl
from jax.experimental.pallas import tpu as pltpu

D_MODEL = 1024
CHUNK = 64
N_META = 16
CONV_W = 256
CONV_K = 3
MLA_HEADS = 8
QK_NOPE = 64
QK_ROPE = 32
V_DIM = 64
MLA_W = MLA_HEADS * V_DIM
Q_LORA = 256
KV_LORA = 128
RWKV_HEAD = 64
RWKV_W = 256
RWKV_HEADS = 4
DECAY_LORA = 64
ICLR_LORA = 64
SHIFT_W = 3 * RWKV_W + DECAY_LORA + ICLR_LORA
ROPE_BASE = 10000.0
RMS_EPS = 1e-6
GN_EPS = 64e-5
NEG = -1e30
LOG2_E = 1.4426950408889634
CHUNK_SHIFT = CHUNK.bit_length() - 1
HEAD_SHIFT = RWKV_HEAD.bit_length() - 1
DECAY_SCALE = 0.6065306597126334

LANES = 128
QTILE = 128
KTILE = 256
FRONT = KTILE - N_META
QK_W = 2 * LANES
ONES_LANE = KV_LORA + QK_ROPE
PROMPT_ROW_TILES = 8
SCAN_GROUP = 4
SCAN_T = 64
SCAN_SPLIT = 2
SCAN_UNROLL = 16
VMEM_LIMIT = 48 * 1024 * 1024

F32 = jnp.float32
BF16 = jnp.bfloat16


def _dot(a, b):
    return jnp.dot(a, b, preferred_element_type=F32)


def _dot_nt(a, b):
    return lax.dot_general(a, b, (((1,), (1,)), ((), ())), preferred_element_type=F32)


def _rms(x, g):
    return x * lax.rsqrt(jnp.mean(x * x, axis=-1, keepdims=True) + RMS_EPS) * g


def _silu(x):
    return x * jax.nn.sigmoid(x)


def _rope_block(x, cos, sin_lo, sin_hi):
    return (x * cos + pltpu.roll(x, LANES - QK_ROPE // 2, 1) * sin_lo
            + pltpu.roll(x, QK_ROPE // 2, 1) * sin_hi)


def _proj_kernel(x_ref, g_ref, wa_ref, wq_ref, wckv_ref, wkr_ref, wgb_ref, wzc_ref, wgc_ref,
                 qg_ref, wuq_ref, bd_ref, kvg_ref, cos_ref, slo_ref, shi_ref,
                 za_ref, qp_ref, kcat_ref, ckv_ref, kr_ref, gb_ref, zc_ref, gc_ref):
    scale = (QK_NOPE + QK_ROPE) ** -0.5 * LOG2_E
    hb = _rms(x_ref[...], g_ref[...]).astype(BF16)
    za_ref[...] = _dot(hb, wa_ref[...])
    gb_ref[...] = _dot(hb, wgb_ref[...])
    zc_ref[...] = _dot(hb, wzc_ref[...])
    gc_ref[...] = _dot(hb, wgc_ref[...])

    cos, slo, shi = cos_ref[...], slo_ref[...], shi_ref[...]
    cqn = _rms(_dot(hb, wq_ref[...]), qg_ref[...]).astype(BF16)
    qe = _dot(cqn, wuq_ref[...])
    nope_w = MLA_HEADS * QK_NOPE
    qabs = _dot(qe[:, :nope_w].astype(BF16), bd_ref[...])
    for h in range(MLA_HEADS):
        lo, hi = h * LANES, (h + 1) * LANES
        rot = _rope_block(qe[:, nope_w + lo:nope_w + hi], cos, slo, shi)
        qp_ref[h, :, 0:LANES] = (qabs[:, lo:hi] * scale).astype(BF16)
        qp_ref[h, :, LANES:QK_W] = (rot * scale).astype(BF16)

    ckvn = _rms(_dot(hb, wckv_ref[...]), kvg_ref[...])
    ckv_ref[...] = ckvn
    krr = _rope_block(_dot(hb, wkr_ref[...]), cos, slo, shi)
    kr_ref[...] = krr[:, :QK_ROPE]
    kcat_ref[:, 0:LANES] = ckvn.astype(BF16)
    ones_lane = lax.broadcasted_iota(jnp.int32, (1, LANES), 1) == ONES_LANE - LANES
    kcat_ref[:, LANES:QK_W] = jnp.where(ones_lane, 1.0, krr).astype(BF16)


def _proj(x, lw, tabs, tm, tab_blocks):
    rows = x.shape[0]
    n = rows // tm
    row = lambda w: pl.BlockSpec((tm, w), lambda i: (i, 0))
    full = lambda a: pl.BlockSpec(a.shape, lambda i: (0,) * a.ndim)
    tab = pl.BlockSpec((tm, LANES), lambda i: (i % tab_blocks, 0))
    weights = (lw["norm_g"], lw["wa"], lw["wq"], lw["wckv"], lw["wkr"], lw["wgb"], lw["wzc"],
               lw["wgc"], lw["q_norm_g"], lw["wuq"], lw["bd"], lw["kv_norm_g"])
    out_shape = (
        jax.ShapeDtypeStruct((rows, 4 * CONV_W), F32),
        jax.ShapeDtypeStruct((MLA_HEADS, rows, QK_W), BF16),
        jax.ShapeDtypeStruct((rows, QK_W), BF16),
        jax.ShapeDtypeStruct((rows, KV_LORA), F32),
        jax.ShapeDtypeStruct((rows, QK_ROPE), F32),
        jax.ShapeDtypeStruct((rows, MLA_W), F32),
        jax.ShapeDtypeStruct((rows, SHIFT_W), F32),
        jax.ShapeDtypeStruct((rows, RWKV_W), F32),
    )
    out_specs = (row(4 * CONV_W), pl.BlockSpec((MLA_HEADS, tm, QK_W), lambda i: (0, i, 0)),
                 row(QK_W), row(KV_LORA), row(QK_ROPE), row(MLA_W), row(SHIFT_W), row(RWKV_W))
    return pl.pallas_call(
        _proj_kernel,
        grid=(n,),
        in_specs=[row(D_MODEL)] + [full(w) for w in weights] + [tab, tab, tab],
        out_specs=out_specs,
        out_shape=out_shape,
        compiler_params=pltpu.CompilerParams(dimension_semantics=("parallel",),
                                             vmem_limit_bytes=VMEM_LIMIT),
        name="proj",
    )(x, *weights, *tabs)


def _heads_out(o_lat, wv_ref, rows):
    out = None
    for h in range(MLA_HEADS):
        part = _dot(o_lat[h * rows:(h + 1) * rows].astype(BF16), wv_ref[h])
        out = part if out is None else out + part
    return out


def _attn_prompt_kernel(q_ref, k_ref, gb_ref, wv_ref, yb_ref, s_ref, mx_ref, acc_ref):
    i = pl.program_id(1)
    rows = MLA_HEADS * QTILE

    @pl.when(i == 0)
    def _():
        yb_ref[...] = jnp.zeros(yb_ref.shape, BF16)

    @pl.when(i > 0)
    def _():
        q = q_ref[...].reshape(rows, QK_W)
        col = lax.broadcasted_iota(jnp.int32, (rows, KTILE), 1)
        qrow = (lax.broadcasted_iota(jnp.int32, (rows, KTILE), 0) & (QTILE - 1)) + i * QTILE
        last = i >> 1

        def run_tiles(start, count, fn):
            def quad(jj, carry):
                fn(start + 4 * jj, 4 * KTILE)
                return carry

            lax.fori_loop(0, count >> 2, quad, 0)
            done = start + (count & ~3)

            @pl.when((count & 2) != 0)
            def _():
                fn(done, 2 * KTILE)

            @pl.when((count & 1) != 0)
            def _():
                fn(done + (count & 2), KTILE)

        def keys(j, width):
            return k_ref[0, pl.ds(pl.multiple_of(j * KTILE, KTILE), width), :]

        def s_at(j, width):
            return s_ref.at[:, pl.ds(pl.multiple_of(j * KTILE, KTILE), width)]

        def fold(x, op):
            out = x[:, :LANES]
            for c in range(1, x.shape[1] // LANES):
                out = op(out, x[:, c * LANES:(c + 1) * LANES])
            return out

        def scores(j, width, mask=None, first=False):
            s = _dot_nt(q, keys(j, width))
            if mask is not None:
                s = jnp.where(mask, s, NEG)
            s_at(j, width)[...] = s
            sm = fold(s, jnp.maximum)
            mx_ref[...] = sm if first else jnp.maximum(mx_ref[...], sm)

        scores(0, KTILE, col >= FRONT, first=True)
        run_tiles(1, jnp.maximum(last - 1, 0), scores)

        @pl.when(last > 0)
        def _():
            scores(last, KTILE,
                   ((col + last * KTILE) >> CHUNK_SHIFT) <= (qrow >> CHUNK_SHIFT))

        m = jnp.broadcast_to(jnp.max(mx_ref[...], axis=-1, keepdims=True), (rows, LANES))
        mx_ref[...] = m
        acc_ref[...] = jnp.zeros((rows, QK_W), F32)

        def weigh(j, width):
            s = s_at(j, width)[...]
            m_ = mx_ref[...]
            pb = jnp.concatenate([jnp.exp2(s[:, c * LANES:(c + 1) * LANES] - m_).astype(BF16)
                                  for c in range(width // LANES)], axis=1)
            acc_ref[...] += _dot(pb, keys(j, width))

        run_tiles(0, last + 1, weigh)

        acc = acc_ref[...]
        o_lat = acc[:, :KV_LORA] / acc[:, ONES_LANE:ONES_LANE + 1]
        yb_ref[...] = (_heads_out(o_lat, wv_ref, QTILE) * _silu(gb_ref[...])).astype(BF16)


def _attn_prompt(qp, kcat, gb, wv, batch, tp):
    nq = tp // QTILE
    rows = MLA_HEADS * QTILE
    return pl.pallas_call(
        _attn_prompt_kernel,
        grid=(batch, nq),
        in_specs=[
            pl.BlockSpec((MLA_HEADS, QTILE, QK_W), lambda b, i: (0, b * nq + i, 0)),
            pl.BlockSpec((1, tp, QK_W), lambda b, i: (b, 0, 0)),
            pl.BlockSpec((QTILE, MLA_W), lambda b, i: (b * nq + i, 0)),
            pl.BlockSpec(wv.shape, lambda b, i: (0, 0, 0)),
        ],
        out_specs=pl.BlockSpec((QTILE, MLA_W), lambda b, i: (b * nq + i, 0)),
        out_shape=jax.ShapeDtypeStruct((batch * tp, MLA_W), BF16),
        scratch_shapes=[pltpu.VMEM((rows, tp), F32),
                        pltpu.VMEM((rows, LANES), F32),
                        pltpu.VMEM((rows, QK_W), F32)],
        compiler_params=pltpu.CompilerParams(dimension_semantics=("parallel", "arbitrary"),
                                             vmem_limit_bytes=VMEM_LIMIT),
        name="attn_prompt",
    )(qp, kcat.reshape(batch, tp, QK_W), gb, wv)


def _attn_sample_kernel(q_ref, kn_ref, ck_ref, kr_ref, gb_ref, wv_ref, yb_ref):
    ts = q_ref.shape[1]
    q = q_ref[...].reshape(MLA_HEADS * ts, QK_W)
    kn = kn_ref[...]
    ck = ck_ref[0, 0].astype(BF16)
    kr = kr_ref[0, 0].astype(BF16)
    s_past = _dot_nt(q[:, :KV_LORA], ck) + _dot_nt(q[:, KV_LORA:KV_LORA + QK_ROPE], kr)
    s_new = _dot_nt(q, kn)
    m = jnp.maximum(jnp.max(s_past, axis=-1, keepdims=True),
                    jnp.max(s_new, axis=-1, keepdims=True))
    p_past = jnp.exp2(s_past - m)
    p_new = jnp.exp2(s_new - m)
    l = jnp.sum(p_past, axis=-1, keepdims=True) + jnp.sum(p_new, axis=-1, keepdims=True)
    o_lat = (_dot(p_past.astype(BF16), ck) + _dot(p_new.astype(BF16), kn[:, :KV_LORA])) / l
    yb_ref[...] = (_heads_out(o_lat, wv_ref, ts) * _silu(gb_ref[...])).astype(BF16)


def _attn_sample(qp, kcat, cache_ckv, cache_krope, layer, gb, wv, batch, ts):
    past = cache_ckv.shape[2]
    return pl.pallas_call(
        _attn_sample_kernel,
        grid=(batch,),
        in_specs=[
            pl.BlockSpec((MLA_HEADS, ts, QK_W), lambda b: (0, b, 0)),
            pl.BlockSpec((ts, QK_W), lambda b: (b, 0)),
            pl.BlockSpec((1, 1, past, KV_LORA), lambda b: (layer, b, 0, 0)),
            pl.BlockSpec((1, 1, past, QK_ROPE), lambda b: (layer, b, 0, 0)),
            pl.BlockSpec((ts, MLA_W), lambda b: (b, 0)),
            pl.BlockSpec(wv.shape, lambda b: (0, 0, 0)),
        ],
        out_specs=pl.BlockSpec((ts, MLA_W), lambda b: (b, 0)),
        out_shape=jax.ShapeDtypeStruct((batch * ts, MLA_W), BF16),
        compiler_params=pltpu.CompilerParams(dimension_semantics=("parallel",),
                                             vmem_limit_bytes=VMEM_LIMIT),
        name="attn_sample",
    )(qp, kcat, cache_ckv, cache_krope, gb, wv)


def _head_sums(x, ones4):
    hi = x.astype(BF16)
    lo = (x - hi.astype(F32)).astype(BF16)
    return _dot(hi, ones4) + _dot(lo, ones4)


def _mix_kernel(za_ref, zc_ref, gc_ref, conv0_ref, shift0_ref, wkv0_ref,
                cw_ref, mu_ref, w0_ref, w2a2_ref, a0_ref, kkw_ref, kaw_ref, bonus_ref,
                lnw_ref, lnb_ref, wtab_ref,
                ya_ref, yc_ref, conv_ref, shift_ref, wkv_ref,
                s_ref, cu_ref, cz_ref, vec_ref, vtl_ref, yt_ref, *, nblk):
    tb = pl.program_id(1)
    tblk = SCAN_T
    npair = SCAN_GROUP * RWKV_HEADS // 2
    hw = RWKV_HEAD

    @pl.when(tb == 0)
    def _():
        for pr in range(npair):
            b, hp = divmod(pr, 2)
            s_ref[pr, :, 0:hw] = wkv0_ref[b, 2 * hp]
            s_ref[pr, :, hw:2 * hw] = wkv0_ref[b, 2 * hp + 1]
        cu_ref[...] = conv0_ref[...]
        cz_ref[...] = shift0_ref[...]

    trow = lax.broadcasted_iota(jnp.int32, (tblk, 1), 0)
    lane128 = lax.broadcasted_iota(jnp.int32, (1, LANES), 1)
    ri = lax.broadcasted_iota(jnp.int32, (2 * LANES, 2 * LANES), 0)
    ci = lax.broadcasted_iota(jnp.int32, (2 * LANES, 2 * LANES), 1)
    ones4 = jnp.where((ri >> HEAD_SHIFT) == (ci >> HEAD_SHIFT), 1.0, 0.0).astype(BF16)
    cw = cw_ref[...]
    zpad = jnp.zeros((LANES - hw, LANES), F32)
    bonus_v = []
    for b in range(SCAN_GROUP):
        za = za_ref[b]
        xin, bg = za[:, 0:CONV_W], za[:, CONV_W:2 * CONV_W]
        cg, ga = za[:, 2 * CONV_W:3 * CONV_W], za[:, 3 * CONV_W:4 * CONV_W]
        u = cg * xin
        cu = cu_ref[b]
        u1 = jnp.where(trow == 0, cu[1:2], pltpu.roll(u, 1, 0))
        u2 = jnp.where(trow == 0, cu[0:1], jnp.where(trow == 1, cu[1:2], pltpu.roll(u, 2, 0)))
        conv = cw[0:1] * u2 + cw[1:2] * u1 + cw[2:3] * u
        ya_ref[b] = (bg * conv * _silu(ga)).astype(BF16)
        cu_ref[b] = u[tblk - (CONV_K - 1):tblk]

        zc = zc_ref[b]
        prev = jnp.where(trow == 0, cz_ref[b], pltpu.roll(zc, 1, 0))
        cz_ref[b] = zc[tblk - 1:tblk]
        zs = zc + (prev - zc) * mu_ref[...]
        r, k, v = zs[:, 0:RWKV_W], zs[:, RWKV_W:2 * RWKV_W], zs[:, 2 * RWKV_W:3 * RWKV_W]
        la = zs[:, 3 * RWKV_W:SHIFT_W]
        la = jnp.where(lane128 < DECAY_LORA, jnp.tanh(la), la)
        lora = _dot(la.astype(BF16), w2a2_ref[...])
        dec_in = w0_ref[...] + lora[:, 0:RWKV_W]
        decay = jnp.exp(-DECAY_SCALE * jax.nn.sigmoid(dec_in))
        a = jax.nn.sigmoid(a0_ref[...] + lora[:, RWKV_W:2 * RWKV_W])
        kk = k * kkw_ref[...]
        kk = kk / jnp.maximum(jnp.sqrt(_head_sums(kk * kk, ones4)), 1e-12)
        kp = k * (1.0 + (a - 1.0) * kaw_ref[...])
        bonus_v.append(_head_sums(r * kp * bonus_ref[...], ones4) * v)

        for n, vec in enumerate((-kk, decay, kk * a, kp, r)):
            vec_ref[n, 2 * b] = vec[:, 0:LANES]
            vec_ref[n, 2 * b + 1] = vec[:, LANES:2 * LANES]
        for hp in range(2):
            z = jnp.concatenate([v[:, hp * LANES:(hp + 1) * LANES], zpad], axis=0).T
            vt = z[0:hw] + pltpu.roll(z[hw:2 * hw], SCAN_T, 1)
            side, n = divmod(2 * b + hp, npair // 2)
            vtl_ref[n * hw:(n + 1) * hw, side * LANES:(side + 1) * LANES] = vt.astype(BF16)

    tok_of_lane = lane128 & (SCAN_T - 1)
    gsz = npair // SCAN_SPLIT
    groups = [list(range(g * gsz, (g + 1) * gsz)) for g in range(SCAN_SPLIT)]

    def lane_sums(s_list, prs, t_a, t_b):
        lhs = jnp.concatenate(
            [jnp.concatenate([(s * vec_ref[0, pr, pl.ds(t_a, 1), :]).astype(BF16),
                              (s * vec_ref[4, pr, pl.ds(t_b, 1), :]).astype(BF16)], axis=1)
             for s, pr in zip(s_list, prs)], axis=0)
        return _dot(lhs, ones4)

    def step(t, carry):
        rowv = lambda n, pr: vec_ref[n, pr, pl.ds(t, 1), :]
        vb = _dot(vtl_ref[...], wtab_ref[t])
        hit = tok_of_lane == t - 1
        for prs in groups:
            s_old = [s_ref[pr] for pr in prs]
            sums = lane_sums(s_old, prs, t, jnp.maximum(t - 1, 0))
            for n, pr in enumerate(prs):
                rows = slice(n * hw, (n + 1) * hw)
                side, m = divmod(pr, npair // 2)
                vbp = vb[m * hw:(m + 1) * hw, side * LANES:(side + 1) * LANES]
                s_ref[pr] = (s_old[n] * rowv(1, pr) + sums[rows, 0:LANES] * rowv(2, pr)
                             + vbp * rowv(3, pr))
                yt_ref[pr] = jnp.where(hit, sums[rows, LANES:2 * LANES], yt_ref[pr])
        return carry

    yt_ref[...] = jnp.zeros(yt_ref.shape, F32)
    lax.fori_loop(0, SCAN_T, step, 0, unroll=SCAN_UNROLL)
    hit = tok_of_lane == SCAN_T - 1
    for prs in groups:
        sums = lane_sums([s_ref[pr] for pr in prs], prs, SCAN_T - 1, SCAN_T - 1)
        for n, pr in enumerate(prs):
            yt_ref[pr] = jnp.where(hit, sums[n * hw:(n + 1) * hw, LANES:2 * LANES], yt_ref[pr])

    for b in range(SCAN_GROUP):
        halves = []
        for hp in range(2):
            z = jnp.concatenate([yt_ref[2 * b + hp], zpad], axis=0).T
            halves.append(z[0:SCAN_T] + pltpu.roll(z[SCAN_T:2 * SCAN_T], hw, 1))
        ys = jnp.concatenate(halves, axis=1)
        mean = _head_sums(ys, ones4) * (1.0 / RWKV_HEAD)
        d = ys - mean
        var = _head_sums(d * d, ones4) * (1.0 / RWKV_HEAD)
        yn = d * lax.rsqrt(var + GN_EPS) * lnw_ref[...] + lnb_ref[...]
        yn = yn + bonus_v[b]
        yc_ref[b] = (yn * _silu(gc_ref[b])).astype(BF16)

    @pl.when(tb == nblk - 1)
    def _():
        conv_ref[...] = cu_ref[...]
        shift_ref[...] = cz_ref[...]
        for pr in range(npair):
            b, hp = divmod(pr, 2)
            wkv_ref[b, 2 * hp] = s_ref[pr, :, 0:hw]
            wkv_ref[b, 2 * hp + 1] = s_ref[pr, :, hw:2 * hw]


def _mix(za, zc, gc, conv0, shift0, wkv0, lw, batch, t):
    groups = batch // SCAN_GROUP
    nblk = t // SCAN_T
    npair = SCAN_GROUP * RWKV_HEADS // 2
    seq = lambda w: pl.BlockSpec((SCAN_GROUP, SCAN_T, w), lambda g, i: (g, i, 0))
    state3 = lambda a: pl.BlockSpec((SCAN_GROUP,) + a.shape[1:], lambda g, i: (g, 0, 0))
    state4 = pl.BlockSpec((SCAN_GROUP, RWKV_HEADS, RWKV_HEAD, RWKV_HEAD),
                          lambda g, i: (g, 0, 0, 0))
    full = lambda a: pl.BlockSpec(a.shape, lambda g, i: (0,) * a.ndim)
    r_, c_ = np.arange(2 * LANES)[:, None], np.arange(2 * LANES)[None, :]
    wtab = jnp.asarray(((r_ & (SCAN_T - 1)) == np.arange(SCAN_T)[:, None, None])
                       & ((r_ >> HEAD_SHIFT) == (c_ >> HEAD_SHIFT)), dtype=BF16)
    params = (lw["conv_w"], lw["shift_mu"], lw["decay_w0"], lw["w2a2"], lw["iclr_a0"],
              lw["key_kk"], lw["key_ka"], lw["bonus_rk"], lw["lnx_w"], lw["lnx_b"], wtab)
    out_shape = (
        jax.ShapeDtypeStruct((batch, t, CONV_W), BF16),
        jax.ShapeDtypeStruct((batch, t, RWKV_W), BF16),
        jax.ShapeDtypeStruct(conv0.shape, F32),
        jax.ShapeDtypeStruct(shift0.shape, F32),
        jax.ShapeDtypeStruct(wkv0.shape, F32),
    )
    return pl.pallas_call(
        functools.partial(_mix_kernel, nblk=nblk),
        grid=(groups, nblk),
        in_specs=[seq(4 * CONV_W), seq(SHIFT_W), seq(RWKV_W), state3(conv0), state3(shift0),
                  state4] + [full(p) for p in params],
        out_specs=(seq(CONV_W), seq(RWKV_W), state3(conv0), state3(shift0), state4),
        out_shape=out_shape,
        scratch_shapes=[
            pltpu.VMEM((npair, RWKV_HEAD, LANES), F32),
            pltpu.VMEM((SCAN_GROUP, CONV_K - 1, CONV_W), F32),
            pltpu.VMEM((SCAN_GROUP, 1, SHIFT_W), F32),
            pltpu.VMEM((5, npair, SCAN_T, LANES), F32),
            pltpu.VMEM((npair // 2 * RWKV_HEAD, 2 * LANES), BF16),
            pltpu.VMEM((npair, RWKV_HEAD, LANES), F32),
        ],
        compiler_params=pltpu.CompilerParams(dimension_semantics=("parallel", "arbitrary"),
                                             vmem_limit_bytes=VMEM_LIMIT),
        name="mix",
    )(za, zc, gc, conv0, shift0, wkv0, *params)


def _out_kernel(ya_ref, yb_ref, yc_ref, x_ref, wa_ref, wb_ref, wc_ref, fg_ref, o_ref, *, final):
    y = (_dot(ya_ref[...], wa_ref[...]) + _dot(yb_ref[...], wb_ref[...])
         + _dot(yc_ref[...], wc_ref[...]))
    y = x_ref[...] + y
    if final:
        y = _rms(y, fg_ref[...])
    o_ref[...] = y


def _out(ya, yb, yc, x, lw, final_g, final, tm):
    rows = x.shape[0]
    row = lambda w: pl.BlockSpec((tm, w), lambda i: (i, 0))
    full = lambda a: pl.BlockSpec(a.shape, lambda i: (0,) * a.ndim)
    weights = (lw["wo_a"], lw["wo_b"], lw["wo_c"], final_g)
    return pl.pallas_call(
        functools.partial(_out_kernel, final=final),
        grid=(rows // tm,),
        in_specs=[row(CONV_W), row(MLA_W), row(RWKV_W), row(D_MODEL)] + [full(w) for w in weights],
        out_specs=row(D_MODEL),
        out_shape=jax.ShapeDtypeStruct((rows, D_MODEL), F32),
        compiler_params=pltpu.CompilerParams(dimension_semantics=("parallel",),
                                             vmem_limit_bytes=VMEM_LIMIT),
        name="out",
    )(ya, yb, yc, x, *weights)


def _layer_weights(l, norm_g, w_in, conv_w, q_norm_g, w_uq, kv_norm_g, w_ukv, shift_mu,
                   decay_w0, decay_w2, iclr_a0, iclr_a2, key_kk, key_ka, bonus_rk, lnx_w, lnx_b,
                   w_out):
    row = lambda p: p[l].reshape(1, -1).astype(F32)
    wi = w_in[l].astype(BF16)
    o = np.cumsum([0, 4 * CONV_W, Q_LORA, KV_LORA, QK_ROPE, MLA_W, SHIFT_W, RWKV_W]).tolist()
    cols = lambda j: wi[:, o[j]:o[j + 1]]
    uq = w_uq[l].astype(BF16).reshape(Q_LORA, MLA_HEADS, QK_NOPE + QK_ROPE)
    uq_rope = jnp.pad(uq[:, :, QK_NOPE:], ((0, 0), (0, 0), (0, LANES - QK_ROPE)))
    ukv = w_ukv[l].astype(BF16).reshape(KV_LORA, MLA_HEADS, QK_NOPE + V_DIM)
    eye = jnp.eye(MLA_HEADS, dtype=BF16)
    bd = jnp.einsum("chd,hg->hdgc", ukv[:, :, :QK_NOPE], eye).reshape(
        MLA_HEADS * QK_NOPE, MLA_HEADS * KV_LORA)
    wv = jnp.einsum("che,hg->hcge", ukv[:, :, QK_NOPE:], eye).reshape(
        MLA_HEADS, KV_LORA, MLA_W)
    zeros = jnp.zeros((DECAY_LORA, RWKV_W), BF16)
    w2a2 = jnp.concatenate([
        jnp.concatenate([decay_w2[l].astype(BF16), zeros], axis=1),
        jnp.concatenate([zeros, iclr_a2[l].astype(BF16)], axis=1)], axis=0)
    wo = w_out[l].astype(BF16)
    return {
        "norm_g": row(norm_g), "wa": cols(0), "wq": cols(1), "wckv": cols(2),
        "wkr": jnp.pad(cols(3), ((0, 0), (0, LANES - QK_ROPE))),
        "wgb": cols(4), "wzc": cols(5), "wgc": cols(6),
        "q_norm_g": row(q_norm_g),
        "wuq": jnp.concatenate([uq[:, :, :QK_NOPE].reshape(Q_LORA, -1),
                                uq_rope.reshape(Q_LORA, -1)], axis=1),
        "bd": bd, "wv": wv, "kv_norm_g": row(kv_norm_g),
        "conv_w": conv_w[l].astype(F32), "shift_mu": row(shift_mu), "decay_w0": row(decay_w0),
        "w2a2": w2a2, "iclr_a0": row(iclr_a0), "key_kk": row(key_kk), "key_ka": row(key_ka),
        "bonus_rk": row(bonus_rk), "lnx_w": row(lnx_w), "lnx_b": row(lnx_b),
        "wo_a": wo[0:CONV_W], "wo_b": wo[CONV_W:CONV_W + MLA_W], "wo_c": wo[CONV_W + MLA_W:],
    }


def _rope_tables(pos):
    half = QK_ROPE // 2
    inv = ROPE_BASE ** (-jnp.arange(half, dtype=F32) * 2.0 / QK_ROPE)
    ang = pos.astype(F32)[:, None] * inv[None, :]
    cos, sin = jnp.cos(ang), jnp.sin(ang)
    z = jnp.zeros((pos.shape[0], LANES - QK_ROPE), F32)
    zh = jnp.zeros((pos.shape[0], half), F32)
    return (jnp.concatenate([cos, cos, z], axis=1),
            jnp.concatenate([-sin, zh, z], axis=1),
            jnp.concatenate([zh, sin, z], axis=1))


def _stream(x, batch, t, tm, tabs, tab_blocks, states, attn, lws, final_g):
    outs = []
    depth = len(lws)
    for l, lw in enumerate(lws):
        za, qp, kcat, ckv, kr, gb, zc, gc = _proj(x, lw, tabs, tm, tab_blocks)
        yb = attn(l, lw, qp, kcat, gb)
        conv0, shift0, wkv0 = states[l]
        ya, yc, conv, shift, wkv = _mix(
            za.reshape(batch, t, -1), zc.reshape(batch, t, -1), gc.reshape(batch, t, -1),
            conv0, shift0, wkv0, lw, batch, t)
        x = _out(ya.reshape(batch * t, -1), yb, yc.reshape(batch * t, -1), x, lw, final_g,
                 l == depth - 1, tm)
        outs.append((ckv.reshape(batch, t, -1), kr.reshape(batch, t, -1), conv,
                     shift.reshape(batch, -1), wkv))
    return x, outs


def kernel(x_prompt, x_sample, cache_ckv, cache_krope, state_conv, state_shift, state_wkv,
           meta_tokens, norm_g, w_in, conv_w, q_norm_g, w_uq, kv_norm_g, w_ukv, shift_mu,
           decay_w0, decay_w2, iclr_a0, iclr_a2, key_kk, key_ka, bonus_rk, lnx_w, lnx_b,
           w_out, final_g):
    depth = w_in.shape[0]
    bp, seq = x_prompt.shape[:2]
    bs, ts = x_sample.shape[:2]
    past = cache_ckv.shape[2]
    tp = FRONT + N_META + seq
    assert tp % KTILE == 0 and bp % SCAN_GROUP == 0 and bs % SCAN_GROUP == 0
    assert seq % CHUNK == 0 and ts == SCAN_T and tp % SCAN_T == 0
    lws = [_layer_weights(l, norm_g, w_in, conv_w, q_norm_g, w_uq, kv_norm_g, w_ukv, shift_mu,
                          decay_w0, decay_w2, iclr_a0, iclr_a2, key_kk, key_ka, bonus_rk,
                          lnx_w, lnx_b, w_out) for l in range(depth)]
    fg = final_g.reshape(1, -1).astype(F32)

    xp = jnp.concatenate([
        jnp.zeros((bp, FRONT, D_MODEL), F32),
        jnp.broadcast_to(meta_tokens.astype(F32)[None], (bp, N_META, D_MODEL)),
        x_prompt], axis=1).reshape(bp * tp, D_MODEL)
    tm_p = tp // PROMPT_ROW_TILES
    tabs_p = _rope_tables(jnp.maximum(jnp.arange(tp, dtype=jnp.int32) - FRONT, 0))
    zero_states = [(jnp.zeros((bp, CONV_K - 1, CONV_W), F32), jnp.zeros((bp, 1, SHIFT_W), F32),
                    jnp.zeros((bp, RWKV_HEADS, RWKV_HEAD, RWKV_HEAD), F32))] * depth
    attn_p = lambda l, lw, qp, kcat, gb: _attn_prompt(qp, kcat, gb, lw["wv"], bp, tp)
    yp, outs_p = _stream(xp, bp, tp, tm_p, tabs_p, tp // tm_p, zero_states, attn_p, lws, fg)

    tm_s = 512
    reps = tm_s // ts
    tabs_s = tuple(jnp.tile(tb_, (reps, 1))
                   for tb_ in _rope_tables(past + jnp.arange(ts, dtype=jnp.int32)))
    states_s = [(state_conv[l], state_shift[l].reshape(bs, 1, SHIFT_W), state_wkv[l])
                for l in range(depth)]
    attn_s = lambda l, lw, qp, kcat, gb: _attn_sample(qp, kcat, cache_ckv, cache_krope, l, gb,
                                                      lw["wv"], bs, ts)
    ys, outs_s = _stream(x_sample.reshape(bs * ts, D_MODEL), bs, ts, tm_s, tabs_s, 1,
                         states_s, attn_s, lws, fg)

    stack = lambda outs, j: jnp.stack([o[j] for o in outs])
    y_prompt = yp.reshape(bp, tp, D_MODEL)[:, FRONT + N_META:]
    ckv_p = stack(outs_p, 0)[:, :, FRONT:]
    kr_p = stack(outs_p, 1)[:, :, FRONT:]
    return (y_prompt, ys.reshape(bs, ts, D_MODEL), ckv_p, kr_p, stack(outs_p, 2),
            stack(outs_p, 3), stack(outs_p, 4),
            stack(outs_s, 0), stack(outs_s, 1), stack(outs_s, 2), stack(outs_s, 3),
            stack(outs_s, 4))
```

```python
import functools

import jax
import jax.numpy as jnp
import numpy as np
from jax import lax
from jax.experimental import pallas as pl
from jax.experimental.pallas import tpu as pltpu

D_MODEL = 1024
CHUNK = 64
N_META = 16
CONV_W = 256
CONV_K = 3
MLA_HEADS = 8
QK_NOPE = 64
QK_ROPE = 32
V_DIM = 64
MLA_W = MLA_HEADS * V_DIM
Q_LORA = 256
KV_LORA = 128
RWKV_HEAD = 64
RWKV_W = 256
RWKV_HEADS = 4
DECAY_LORA = 64
ICLR_LORA = 64
SHIFT_W = 3 * RWKV_W + DECAY_LORA + ICLR_LORA
ROPE_BASE = 10000.0
RMS_EPS = 1e-6
GN_EPS = 64e-5
NEG = -1e30
LOG2_E = 1.4426950408889634
CHUNK_SHIFT = CHUNK.bit_length() - 1
HEAD_SHIFT = RWKV_HEAD.bit_length() - 1
DECAY_SCALE = 0.6065306597126334

LANES = 128
QTILE = 256
KTILE = 256
FRONT = KTILE - N_META
QK_W = 2 * LANES
ONES_LANE = KV_LORA + QK_ROPE
PAD_LANE = ONES_LANE + 1
PROMPT_ROW_TILES = 8
SCAN_GROUP = 4
SCAN_T = 64
SCAN_SPLIT = 2
SCAN_UNROLL = 16
VMEM_LIMIT = 48 * 1024 * 1024
ATTN_VMEM_LIMIT = 58 * 1024 * 1024

F32 = jnp.float32
BF16 = jnp.bfloat16


def _dot(a, b):
    return jnp.dot(a, b, preferred_element_type=F32)


def _dot_nt(a, b):
    return lax.dot_general(a, b, (((1,), (1,)), ((), ())), preferred_element_type=F32)


def _rms(x, g):
    return x * lax.rsqrt(jnp.mean(x * x, axis=-1, keepdims=True) + RMS_EPS) * g


def _silu(x):
    return x * jax.nn.sigmoid(x)


def _rope_block(x, cos, sin_lo, sin_hi):
    return (x * cos + pltpu.roll(x, LANES - QK_ROPE // 2, 1) * sin_lo
            + pltpu.roll(x, QK_ROPE // 2, 1) * sin_hi)


def _proj_kernel(x_ref, g_ref, wa_ref, wq_ref, wckv_ref, wkr_ref, wgb_ref, wzc_ref, wgc_ref,
                 qg_ref, wuq_ref, bd_ref, kvg_ref, cos_ref, slo_ref, shi_ref,
                 za_ref, qp_ref, kcat_ref, ckv_ref, kr_ref, gb_ref, zc_ref, gc_ref):
    scale = (QK_NOPE + QK_ROPE) ** -0.5 * LOG2_E
    hb = _rms(x_ref[...], g_ref[...]).astype(BF16)
    za_ref[...] = _dot(hb, wa_ref[...])
    gb_ref[...] = _dot(hb, wgb_ref[...])
    zc_ref[...] = _dot(hb, wzc_ref[...])
    gc_ref[...] = _dot(hb, wgc_ref[...])

    cos, slo, shi = cos_ref[...], slo_ref[...], shi_ref[...]
    cqn = _rms(_dot(hb, wq_ref[...]), qg_ref[...]).astype(BF16)
    qe = _dot(cqn, wuq_ref[...])
    nope_w = MLA_HEADS * QK_NOPE
    qabs = _dot(qe[:, :nope_w].astype(BF16), bd_ref[...])
    lane = lax.broadcasted_iota(jnp.int32, (1, LANES), 1)
    ones_lane, pad_lane = lane == ONES_LANE - LANES, lane == PAD_LANE - LANES
    for h in range(MLA_HEADS):
        lo, hi = h * LANES, (h + 1) * LANES
        rot = _rope_block(qe[:, nope_w + lo:nope_w + hi], cos, slo, shi)
        qp_ref[h, :, 0:LANES] = (qabs[:, lo:hi] * scale).astype(BF16)
        qp_ref[h, :, LANES:QK_W] = jnp.where(pad_lane, NEG, rot * scale).astype(BF16)

    ckvn = _rms(_dot(hb, wckv_ref[...]), kvg_ref[...])
    ckv_ref[...] = ckvn
    krr = _rope_block(_dot(hb, wkr_ref[...]), cos, slo, shi)
    kr_ref[...] = krr[:, :QK_ROPE]
    kcat_ref[:, 0:LANES] = ckvn.astype(BF16)
    krr = jnp.where(ones_lane, 1.0, jnp.where(pad_lane, cos, krr))
    kcat_ref[:, LANES:QK_W] = krr.astype(BF16)


def _proj(x, lw, tabs, tm, tab_blocks):
    rows = x.shape[0]
    n = rows // tm
    row = lambda w: pl.BlockSpec((tm, w), lambda i: (i, 0))
    full = lambda a: pl.BlockSpec(a.shape, lambda i: (0,) * a.ndim)
    tab = pl.BlockSpec((tm, LANES), lambda i: (i % tab_blocks, 0))
    weights = (lw["norm_g"], lw["wa"], lw["wq"], lw["wckv"], lw["wkr"], lw["wgb"], lw["wzc"],
               lw["wgc"], lw["q_norm_g"], lw["wuq"], lw["bd"], lw["kv_norm_g"])
    out_shape = (
        jax.ShapeDtypeStruct((rows, 4 * CONV_W), F32),
        jax.ShapeDtypeStruct((MLA_HEADS, rows, QK_W), BF16),
        jax.ShapeDtypeStruct((rows, QK_W), BF16),
        jax.ShapeDtypeStruct((rows, KV_LORA), F32),
        jax.ShapeDtypeStruct((rows, QK_ROPE), F32),
        jax.ShapeDtypeStruct((rows, MLA_W), F32),
        jax.ShapeDtypeStruct((rows, SHIFT_W), F32),
        jax.ShapeDtypeStruct((rows, RWKV_W), F32),
    )
    out_specs = (row(4 * CONV_W), pl.BlockSpec((MLA_HEADS, tm, QK_W), lambda i: (0, i, 0)),
                 row(QK_W), row(KV_LORA), row(QK_ROPE), row(MLA_W), row(SHIFT_W), row(RWKV_W))
    return pl.pallas_call(
        _proj_kernel,
        grid=(n,),
        in_specs=[row(D_MODEL)] + [full(w) for w in weights] + [tab, tab, tab],
        out_specs=out_specs,
        out_shape=out_shape,
        compiler_params=pltpu.CompilerParams(dimension_semantics=("parallel",),
                                             vmem_limit_bytes=VMEM_LIMIT),
        name="proj",
    )(x, *weights, *tabs)


def _heads_out(o_lat, wv_ref, rows):
    out = None
    for h in range(MLA_HEADS):
        part = _dot(o_lat[h * rows:(h + 1) * rows].astype(BF16), wv_ref[h])
        out = part if out is None else out + part
    return out


def _attn_prompt_kernel(q_ref, k_ref, gb_ref, wv_ref, yb_ref, s_ref, mx_ref, acc_ref):
    i = pl.program_id(1)
    rows = MLA_HEADS * QTILE

    all_front = (i + 1) * QTILE <= FRONT

    @pl.when(all_front)
    def _():
        yb_ref[...] = jnp.zeros(yb_ref.shape, BF16)

    @pl.when(jnp.logical_not(all_front))
    def _():
        q = q_ref[...].reshape(rows, QK_W)
        lane = lax.broadcasted_iota(jnp.int32, (rows, LANES), 1)
        qrow = (lax.broadcasted_iota(jnp.int32, (rows, LANES), 0) & (QTILE - 1)) + i * QTILE
        last = i * (QTILE // KTILE)

        def run_tiles(start, count, fn):
            def quad(jj, carry):
                fn(start + 4 * jj, 4 * KTILE)
                return carry

            lax.fori_loop(0, count >> 2, quad, 0)
            done = start + (count & ~3)

            @pl.when((count & 2) != 0)
            def _():
                fn(done, 2 * KTILE)

            @pl.when((count & 1) != 0)
            def _():
                fn(done + (count & 2), KTILE)

        def keys(j, width):
            return k_ref[0, pl.ds(pl.multiple_of(j * KTILE, KTILE), width), :]

        def s_at(j, width):
            return s_ref.at[:, pl.ds(pl.multiple_of(j * KTILE, KTILE), width)]

        def fold(x, op):
            out = x[:, :LANES]
            for c in range(1, x.shape[1] // LANES):
                out = op(out, x[:, c * LANES:(c + 1) * LANES])
            return out

        last_visible = qrow | (CHUNK - 1)
        mx_ref[...] = jnp.full((rows, LANES), NEG, F32)

        def scores(j, width):
            s = _dot_nt(q, keys(j, width))
            blocks = []
            for c in range(width // LANES):
                key_row = lane + (j * KTILE + c * LANES)
                blocks.append(jnp.where(key_row <= last_visible,
                                        s[:, c * LANES:(c + 1) * LANES], NEG))
            s = jnp.concatenate(blocks, axis=1)
            s_at(j, width)[...] = s
            mx_ref[...] = jnp.maximum(mx_ref[...], fold(s, jnp.maximum))

        run_tiles(0, last + 1, scores)

        m = jnp.broadcast_to(jnp.max(mx_ref[...], axis=-1, keepdims=True), (rows, LANES))
        mx_ref[...] = m
        acc_ref[...] = jnp.zeros((rows, QK_W), F32)

        def weigh(j, width):
            s = s_at(j, width)[...]
            m_ = mx_ref[...]
            pb = jnp.concatenate([jnp.exp2(s[:, c * LANES:(c + 1) * LANES] - m_).astype(BF16)
                                  for c in range(width // LANES)], axis=1)
            acc_ref[...] += _dot(pb, keys(j, width))

        run_tiles(0, last + 1, weigh)

        acc = acc_ref[...]
        o_lat = acc[:, :KV_LORA] / acc[:, ONES_LANE:ONES_LANE + 1]
        yb_ref[...] = (_heads_out(o_lat, wv_ref, QTILE) * _silu(gb_ref[...])).astype(BF16)


def _attn_prompt(qp, kcat, gb, wv, batch, tp):
    nq = tp // QTILE
    rows = MLA_HEADS * QTILE
    return pl.pallas_call(
        _attn_prompt_kernel,
        grid=(batch, nq),
        in_specs=[
            pl.BlockSpec((MLA_HEADS, QTILE, QK_W), lambda b, i: (0, b * nq + i, 0)),
            pl.BlockSpec((1, tp, QK_W), lambda b, i: (b, 0, 0)),
            pl.BlockSpec((QTILE, MLA_W), lambda b, i: (b * nq + i, 0)),
            pl.BlockSpec(wv.shape, lambda b, i: (0, 0, 0)),
        ],
        out_specs=pl.BlockSpec((QTILE, MLA_W), lambda b, i: (b * nq + i, 0)),
        out_shape=jax.ShapeDtypeStruct((batch * tp, MLA_W), BF16),
        scratch_shapes=[pltpu.VMEM((rows, tp), F32),
                        pltpu.VMEM((rows, LANES), F32),
                        pltpu.VMEM((rows, QK_W), F32)],
        compiler_params=pltpu.CompilerParams(dimension_semantics=("parallel", "arbitrary"),
                                             vmem_limit_bytes=ATTN_VMEM_LIMIT),
        name="attn_prompt",
    )(qp, kcat.reshape(batch, tp, QK_W), gb, wv)


def _attn_sample_kernel(q_ref, kn_ref, ck_ref, kr_ref, gb_ref, wv_ref, yb_ref):
    ts = q_ref.shape[1]
    q = q_ref[...].reshape(MLA_HEADS * ts, QK_W)
    kn = kn_ref[...]
    ck = ck_ref[0, 0].astype(BF16)
    kr = kr_ref[0, 0].astype(BF16)
    s_past = _dot_nt(q[:, :KV_LORA], ck) + _dot_nt(q[:, KV_LORA:KV_LORA + QK_ROPE], kr)
    s_new = _dot_nt(q, kn)
    m = jnp.maximum(jnp.max(s_past, axis=-1, keepdims=True),
                    jnp.max(s_new, axis=-1, keepdims=True))
    p_past = jnp.exp2(s_past - m)
    p_new = jnp.exp2(s_new - m)
    l = jnp.sum(p_past, axis=-1, keepdims=True) + jnp.sum(p_new, axis=-1, keepdims=True)
    o_lat = (_dot(p_past.astype(BF16), ck) + _dot(p_new.astype(BF16), kn[:, :KV_LORA])) / l
    yb_ref[...] = (_heads_out(o_lat, wv_ref, ts) * _silu(gb_ref[...])).astype(BF16)


def _attn_sample(qp, kcat, cache_ckv, cache_krope, layer, gb, wv, batch, ts):
    past = cache_ckv.shape[2]
    return pl.pallas_call(
        _attn_sample_kernel,
        grid=(batch,),
        in_specs=[
            pl.BlockSpec((MLA_HEADS, ts, QK_W), lambda b: (0, b, 0)),
            pl.BlockSpec((ts, QK_W), lambda b: (b, 0)),
            pl.BlockSpec((1, 1, past, KV_LORA), lambda b: (layer, b, 0, 0)),
            pl.BlockSpec((1, 1, past, QK_ROPE), lambda b: (layer, b, 0, 0)),
            pl.BlockSpec((ts, MLA_W), lambda b: (b, 0)),
            pl.BlockSpec(wv.shape, lambda b: (0, 0, 0)),
        ],
        out_specs=pl.BlockSpec((ts, MLA_W), lambda b: (b, 0)),
        out_shape=jax.ShapeDtypeStruct((batch * ts, MLA_W), BF16),
        compiler_params=pltpu.CompilerParams(dimension_semantics=("parallel",),
                                             vmem_limit_bytes=VMEM_LIMIT),
        name="attn_sample",
    )(qp, kcat, cache_ckv, cache_krope, gb, wv)


def _head_sums(x, ones4):
    hi = x.astype(BF16)
    lo = (x - hi.astype(F32)).astype(BF16)
    both = _dot(jnp.concatenate([hi, lo], axis=0), ones4)
    return both[0:x.shape[0]] + both[x.shape[0]:]


def _mix_kernel(za_ref, zc_ref, gc_ref, conv0_ref, shift0_ref, wkv0_ref,
                cw_ref, mu_ref, w0_ref, w2a2_ref, a0_ref, kkw_ref, kaw_ref, bonus_ref,
                lnw_ref, lnb_ref,
                ya_ref, yc_ref, conv_ref, shift_ref, wkv_ref,
                s_ref, cu_ref, cz_ref, vec_ref, vt_ref, yt_ref, *, nblk):
    tb = pl.program_id(1)
    tblk = SCAN_T
    npair = SCAN_GROUP * RWKV_HEADS // 2
    hw = RWKV_HEAD

    @pl.when(tb == 0)
    def _():
        for pr in range(npair):
            b, hp = divmod(pr, 2)
            s_ref[pr, :, 0:hw] = wkv0_ref[b, 2 * hp]
            s_ref[pr, :, hw:2 * hw] = wkv0_ref[b, 2 * hp + 1]
        cu_ref[...] = conv0_ref[...]
        cz_ref[...] = shift0_ref[...]

    trow = lax.broadcasted_iota(jnp.int32, (tblk, 1), 0)
    lane128 = lax.broadcasted_iota(jnp.int32, (1, LANES), 1)
    ri = lax.broadcasted_iota(jnp.int32, (2 * LANES, 2 * LANES), 0)
    ci = lax.broadcasted_iota(jnp.int32, (2 * LANES, 2 * LANES), 1)
    ones4 = jnp.where((ri >> HEAD_SHIFT) == (ci >> HEAD_SHIFT), 1.0, 0.0).astype(BF16)
    cw = cw_ref[...]
    zpad = jnp.zeros((LANES - hw, LANES), F32)
    streams = range(SCAN_GROUP)
    cat = lambda parts: jnp.concatenate(parts, axis=0)
    cut = lambda x, b: x[b * tblk:(b + 1) * tblk]
    rs, ks, vs, las = [], [], [], []
    for b in streams:
        za = za_ref[b]
        xin, bg = za[:, 0:CONV_W], za[:, CONV_W:2 * CONV_W]
        cg, ga = za[:, 2 * CONV_W:3 * CONV_W], za[:, 3 * CONV_W:4 * CONV_W]
        u = cg * xin
        cu = cu_ref[b]
        u1 = jnp.where(trow == 0, cu[1:2], pltpu.roll(u, 1, 0))
        u2 = jnp.where(trow == 0, cu[0:1], jnp.where(trow == 1, cu[1:2], pltpu.roll(u, 2, 0)))
        conv = cw[0:1] * u2 + cw[1:2] * u1 + cw[2:3] * u
        ya_ref[b] = (bg * conv * _silu(ga)).astype(BF16)
        cu_ref[b] = u[tblk - (CONV_K - 1):tblk]

        zc = zc_ref[b]
        prev = jnp.where(trow == 0, cz_ref[b], pltpu.roll(zc, 1, 0))
        cz_ref[b] = zc[tblk - 1:tblk]
        zs = zc + (prev - zc) * mu_ref[...]
        r, k, v = zs[:, 0:RWKV_W], zs[:, RWKV_W:2 * RWKV_W], zs[:, 2 * RWKV_W:3 * RWKV_W]
        la = zs[:, 3 * RWKV_W:SHIFT_W]
        las.append(jnp.where(lane128 < DECAY_LORA, jnp.tanh(la), la).astype(BF16))
        rs.append(r), ks.append(k), vs.append(v)

    lora = _dot(cat(las), w2a2_ref[...])
    decays, avs, kks, kps = [], [], [], []
    for b in streams:
        dec_in = w0_ref[...] + cut(lora, b)[:, 0:RWKV_W]
        decays.append(jnp.exp(-DECAY_SCALE * jax.nn.sigmoid(dec_in)))
        a = jax.nn.sigmoid(a0_ref[...] + cut(lora, b)[:, RWKV_W:2 * RWKV_W])
        avs.append(a)
        kks.append(ks[b] * kkw_ref[...])
        kps.append(ks[b] * (1.0 + (a - 1.0) * kaw_ref[...]))
    kk_sq = _head_sums(cat([kk * kk for kk in kks]), ones4)
    rk_sum = _head_sums(cat([rs[b] * kps[b] * bonus_ref[...] for b in streams]), ones4)
    bonus_v = []
    for b in streams:
        r, v, a, kp, decay = rs[b], vs[b], avs[b], kps[b], decays[b]
        kk = kks[b] / jnp.maximum(jnp.sqrt(cut(kk_sq, b)), 1e-12)
        bonus_v.append(cut(rk_sum, b) * v)
        for n, vec in enumerate((-kk, decay, kk * a, kp, r)):
            vec_ref[n, 2 * b] = vec[:, 0:LANES]
            vec_ref[n, 2 * b + 1] = vec[:, LANES:2 * LANES]
        for hp in range(2):
            z = jnp.concatenate([v[:, hp * LANES:(hp + 1) * LANES], zpad], axis=0).T
            vt_ref[2 * b + hp] = z[0:hw] + pltpu.roll(z[hw:2 * hw], SCAN_T, 1)

    tok_of_lane = lane128 & (SCAN_T - 1)
    head_base = lax.broadcasted_iota(jnp.int32, (hw, LANES), 1) & RWKV_HEAD
    gsz = npair // SCAN_SPLIT
    groups = [list(range(g * gsz, (g + 1) * gsz)) for g in range(SCAN_SPLIT)]

    def lane_sums(s_list, prs, t_a, t_b):
        lhs = jnp.concatenate(
            [jnp.concatenate([(s * vec_ref[0, pr, pl.ds(t_a, 1), :]).astype(BF16),
                              (s * vec_ref[4, pr, pl.ds(t_b, 1), :]).astype(BF16)], axis=1)
             for s, pr in zip(s_list, prs)], axis=0)
        return _dot(lhs, ones4)

    def step(t, carry):
        rowv = lambda n, pr: vec_ref[n, pr, pl.ds(t, 1), :]
        pick = head_base + t
        hit = tok_of_lane == t - 1
        for prs in groups:
            s_old = [s_ref[pr] for pr in prs]
            sums = lane_sums(s_old, prs, t, jnp.maximum(t - 1, 0))
            for n, pr in enumerate(prs):
                rows = slice(n * hw, (n + 1) * hw)
                vb = jnp.take_along_axis(vt_ref[pr], pick, axis=1, mode="promise_in_bounds")
                s_ref[pr] = (s_old[n] * rowv(1, pr) + sums[rows, 0:LANES] * rowv(2, pr)
                             + vb * rowv(3, pr))
                yt_ref[pr] = jnp.where(hit, sums[rows, LANES:2 * LANES], yt_ref[pr])
        return carry

    yt_ref[...] = jnp.zeros(yt_ref.shape, F32)
    lax.fori_loop(0, SCAN_T, step, 0, unroll=SCAN_UNROLL)
    hit = tok_of_lane == SCAN_T - 1
    for prs in groups:
        sums = lane_sums([s_ref[pr] for pr in prs], prs, SCAN_T - 1, SCAN_T - 1)
        for n, pr in enumerate(prs):
            yt_ref[pr] = jnp.where(hit, sums[n * hw:(n + 1) * hw, LANES:2 * LANES], yt_ref[pr])

    ys = []
    for b in streams:
        halves = []
        for hp in range(2):
            z = jnp.concatenate([yt_ref[2 * b + hp], zpad], axis=0).T
            halves.append(z[0:SCAN_T] + pltpu.roll(z[SCAN_T:2 * SCAN_T], hw, 1))
        ys.append(jnp.concatenate(halves, axis=1))
    ys = cat(ys)
    d = ys - _head_sums(ys, ones4) * (1.0 / RWKV_HEAD)
    var = _head_sums(d * d, ones4) * (1.0 / RWKV_HEAD)
    yn = d * lax.rsqrt(var + GN_EPS) * lnw_ref[...] + lnb_ref[...]
    for b in streams:
        yc_ref[b] = ((cut(yn, b) + bonus_v[b]) * _silu(gc_ref[b])).astype(BF16)

    @pl.when(tb == nblk - 1)
    def _():
        conv_ref[...] = cu_ref[...]
        shift_ref[...] = cz_ref[...]
        for pr in range(npair):
            b, hp = divmod(pr, 2)
            wkv_ref[b, 2 * hp] = s_ref[pr, :, 0:hw]
            wkv_ref[b, 2 * hp + 1] = s_ref[pr, :, hw:2 * hw]


def _mix(za, zc, gc, conv0, shift0, wkv0, lw, batch, t):
    groups = batch // SCAN_GROUP
    nblk = t // SCAN_T
    npair = SCAN_GROUP * RWKV_HEADS // 2
    seq = lambda w: pl.BlockSpec((SCAN_GROUP, SCAN_T, w), lambda g, i: (g, i, 0))
    state3 = lambda a: pl.BlockSpec((SCAN_GROUP,) + a.shape[1:], lambda g, i: (g, 0, 0))
    state4 = pl.BlockSpec((SCAN_GROUP, RWKV_HEADS, RWKV_HEAD, RWKV_HEAD),
                          lambda g, i: (g, 0, 0, 0))
    full = lambda a: pl.BlockSpec(a.shape, lambda g, i: (0,) * a.ndim)
    params = (lw["conv_w"], lw["shift_mu"], lw["decay_w0"], lw["w2a2"], lw["iclr_a0"],
              lw["key_kk"], lw["key_ka"], lw["bonus_rk"], lw["lnx_w"], lw["lnx_b"])
    out_shape = (
        jax.ShapeDtypeStruct((batch, t, CONV_W), BF16),
        jax.ShapeDtypeStruct((batch, t, RWKV_W), BF16),
        jax.ShapeDtypeStruct(conv0.shape, F32),
        jax.ShapeDtypeStruct(shift0.shape, F32),
        jax.ShapeDtypeStruct(wkv0.shape, F32),
    )
    return pl.pallas_call(
        functools.partial(_mix_kernel, nblk=nblk),
        grid=(groups, nblk),
        in_specs=[seq(4 * CONV_W), seq(SHIFT_W), seq(RWKV_W), state3(conv0), state3(shift0),
                  state4] + [full(p) for p in params],
        out_specs=(seq(CONV_W), seq(RWKV_W), state3(conv0), state3(shift0), state4),
        out_shape=out_shape,
        scratch_shapes=[
            pltpu.VMEM((npair, RWKV_HEAD, LANES), F32),
            pltpu.VMEM((SCAN_GROUP, CONV_K - 1, CONV_W), F32),
            pltpu.VMEM((SCAN_GROUP, 1, SHIFT_W), F32),
            pltpu.VMEM((5, npair, SCAN_T, LANES), F32),
            pltpu.VMEM((npair, RWKV_HEAD, LANES), F32),
            pltpu.VMEM((npair, RWKV_HEAD, LANES), F32),
        ],
        compiler_params=pltpu.CompilerParams(dimension_semantics=("parallel", "arbitrary"),
                                             vmem_limit_bytes=VMEM_LIMIT),
        name="mix",
    )(za, zc, gc, conv0, shift0, wkv0, *params)


def _out_kernel(ya_ref, yb_ref, yc_ref, x_ref, wa_ref, wb_ref, wc_ref, fg_ref, o_ref, *, final):
    y = (_dot(ya_ref[...], wa_ref[...]) + _dot(yb_ref[...], wb_ref[...])
         + _dot(yc_ref[...], wc_ref[...]))
    y = x_ref[...] + y
    if final:
        y = _rms(y, fg_ref[...])
    o_ref[...] = y


def _out(ya, yb, yc, x, lw, final_g, final, tm):
    rows = x.shape[0]
    row = lambda w: pl.BlockSpec((tm, w), lambda i: (i, 0))
    full = lambda a: pl.BlockSpec(a.shape, lambda i: (0,) * a.ndim)
    weights = (lw["wo_a"], lw["wo_b"], lw["wo_c"], final_g)
    return pl.pallas_call(
        functools.partial(_out_kernel, final=final),
        grid=(rows // tm,),
        in_specs=[row(CONV_W), row(MLA_W), row(RWKV_W), row(D_MODEL)] + [full(w) for w in weights],
        out_specs=row(D_MODEL),
        out_shape=jax.ShapeDtypeStruct((rows, D_MODEL), F32),
        compiler_params=pltpu.CompilerParams(dimension_semantics=("parallel",),
                                             vmem_limit_bytes=VMEM_LIMIT),
        name="out",
    )(ya, yb, yc, x, *weights)


def _layer_weights(l, norm_g, w_in, conv_w, q_norm_g, w_uq, kv_norm_g, w_ukv, shift_mu,
                   decay_w0, decay_w2, iclr_a0, iclr_a2, key_kk, key_ka, bonus_rk, lnx_w, lnx_b,
                   w_out):
    row = lambda p: p[l].reshape(1, -1).astype(F32)
    wi = w_in[l].astype(BF16)
    o = np.cumsum([0, 4 * CONV_W, Q_LORA, KV_LORA, QK_ROPE, MLA_W, SHIFT_W, RWKV_W]).tolist()
    cols = lambda j: wi[:, o[j]:o[j + 1]]
    uq = w_uq[l].astype(BF16).reshape(Q_LORA, MLA_HEADS, QK_NOPE + QK_ROPE)
    uq_rope = jnp.pad(uq[:, :, QK_NOPE:], ((0, 0), (0, 0), (0, LANES - QK_ROPE)))
    ukv = w_ukv[l].astype(BF16).reshape(KV_LORA, MLA_HEADS, QK_NOPE + V_DIM)
    eye = jnp.eye(MLA_HEADS, dtype=BF16)
    bd = jnp.einsum("chd,hg->hdgc", ukv[:, :, :QK_NOPE], eye).reshape(
        MLA_HEADS * QK_NOPE, MLA_HEADS * KV_LORA)
    wv = jnp.einsum("che,hg->hcge", ukv[:, :, QK_NOPE:], eye).reshape(
        MLA_HEADS, KV_LORA, MLA_W)
    zeros = jnp.zeros((DECAY_LORA, RWKV_W), BF16)
    w2a2 = jnp.concatenate([
        jnp.concatenate([decay_w2[l].astype(BF16), zeros], axis=1),
        jnp.concatenate([zeros, iclr_a2[l].astype(BF16)], axis=1)], axis=0)
    wo = w_out[l].astype(BF16)
    return {
        "norm_g": row(norm_g), "wa": cols(0), "wq": cols(1), "wckv": cols(2),
        "wkr": jnp.pad(cols(3), ((0, 0), (0, LANES - QK_ROPE))),
        "wgb": cols(4), "wzc": cols(5), "wgc": cols(6),
        "q_norm_g": row(q_norm_g),
        "wuq": jnp.concatenate([uq[:, :, :QK_NOPE].reshape(Q_LORA, -1),
                                uq_rope.reshape(Q_LORA, -1)], axis=1),
        "bd": bd, "wv": wv, "kv_norm_g": row(kv_norm_g),
        "conv_w": conv_w[l].astype(F32), "shift_mu": row(shift_mu), "decay_w0": row(decay_w0),
        "w2a2": w2a2, "iclr_a0": row(iclr_a0), "key_kk": row(key_kk), "key_ka": row(key_ka),
        "bonus_rk": row(bonus_rk), "lnx_w": row(lnx_w), "lnx_b": row(lnx_b),
        "wo_a": wo[0:CONV_W], "wo_b": wo[CONV_W:CONV_W + MLA_W], "wo_c": wo[CONV_W + MLA_W:],
    }


def _rope_tables(pos, front):
    half = QK_ROPE // 2
    n = pos.shape[0]
    inv = ROPE_BASE ** (-jnp.arange(half, dtype=F32) * 2.0 / QK_ROPE)
    ang = pos.astype(F32)[:, None] * inv[None, :]
    cos, sin = jnp.cos(ang), jnp.sin(ang)
    z = jnp.zeros((n, LANES - QK_ROPE), F32)
    zh = jnp.zeros((n, half), F32)
    flag = (jnp.arange(n) < front).astype(F32)[:, None]
    pad_col = PAD_LANE - LANES - QK_ROPE
    zflag = jnp.concatenate([z[:, :pad_col], flag, z[:, pad_col + 1:]], axis=1)
    return (jnp.concatenate([cos, cos, zflag], axis=1),
            jnp.concatenate([-sin, zh, z], axis=1),
            jnp.concatenate([zh, sin, z], axis=1))


def _stream(x, batch, t, tm, tabs, tab_blocks, states, attn, lws, final_g):
    outs = []
    depth = len(lws)
    for l, lw in enumerate(lws):
        za, qp, kcat, ckv, kr, gb, zc, gc = _proj(x, lw, tabs, tm, tab_blocks)
        yb = attn(l, lw, qp, kcat, gb)
        conv0, shift0, wkv0 = states[l]
        ya, yc, conv, shift, wkv = _mix(
            za.reshape(batch, t, -1), zc.reshape(batch, t, -1), gc.reshape(batch, t, -1),
            conv0, shift0, wkv0, lw, batch, t)
        x = _out(ya.reshape(batch * t, -1), yb, yc.reshape(batch * t, -1), x, lw, final_g,
                 l == depth - 1, tm)
        outs.append((ckv.reshape(batch, t, -1), kr.reshape(batch, t, -1), conv,
                     shift.reshape(batch, -1), wkv))
    return x, outs


def kernel(x_prompt, x_sample, cache_ckv, cache_krope, state_conv, state_shift, state_wkv,
           meta_tokens, norm_g, w_in, conv_w, q_norm_g, w_uq, kv_norm_g, w_ukv, shift_mu,
           decay_w0, decay_w2, iclr_a0, iclr_a2, key_kk, key_ka, bonus_rk, lnx_w, lnx_b,
           w_out, final_g):
    depth = w_in.shape[0]
    bp, seq = x_prompt.shape[:2]
    bs, ts = x_sample.shape[:2]
    past = cache_ckv.shape[2]
    tp = FRONT + N_META + seq
    assert tp % KTILE == 0 and bp % SCAN_GROUP == 0 and bs % SCAN_GROUP == 0
    assert seq % CHUNK == 0 and ts == SCAN_T and tp % SCAN_T == 0
    lws = [_layer_weights(l, norm_g, w_in, conv_w, q_norm_g, w_uq, kv_norm_g, w_ukv, shift_mu,
                          decay_w0, decay_w2, iclr_a0, iclr_a2, key_kk, key_ka, bonus_rk,
                          lnx_w, lnx_b, w_out) for l in range(depth)]
    fg = final_g.reshape(1, -1).astype(F32)

    xp = jnp.concatenate([
        jnp.zeros((bp, FRONT, D_MODEL), F32),
        jnp.broadcast_to(meta_tokens.astype(F32)[None], (bp, N_META, D_MODEL)),
        x_prompt], axis=1).reshape(bp * tp, D_MODEL)
    tm_p = tp // PROMPT_ROW_TILES
    tabs_p = _rope_tables(jnp.maximum(jnp.arange(tp, dtype=jnp.int32) - FRONT, 0), FRONT)
    zero_states = [(jnp.zeros((bp, CONV_K - 1, CONV_W), F32), jnp.zeros((bp, 1, SHIFT_W), F32),
                    jnp.zeros((bp, RWKV_HEADS, RWKV_HEAD, RWKV_HEAD), F32))] * depth
    attn_p = lambda l, lw, qp, kcat, gb: _attn_prompt(qp, kcat, gb, lw["wv"], bp, tp)
    yp, outs_p = _stream(xp, bp, tp, tm_p, tabs_p, tp // tm_p, zero_states, attn_p, lws, fg)

    tm_s = 512
    reps = tm_s // ts
    tabs_s = tuple(jnp.tile(tb_, (reps, 1))
                   for tb_ in _rope_tables(past + jnp.arange(ts, dtype=jnp.int32), 0))
    states_s = [(state_conv[l], state_shift[l].reshape(bs, 1, SHIFT_W), state_wkv[l])
                for l in range(depth)]
    attn_s = lambda l, lw, qp, kcat, gb: _attn_sample(qp, kcat, cache_ckv, cache_krope, l, gb,
                                                      lw["wv"], bs, ts)
    ys, outs_s = _stream(x_sample.reshape(bs * ts, D_MODEL), bs, ts, tm_s, tabs_s, 1,
                         states_s, attn_s, lws, fg)

    stack = lambda outs, j: jnp.stack([o[j] for o in outs])
    y_prompt = yp.reshape(bp, tp, D_MODEL)[:, FRONT + N_META:]
    ckv_p = stack(outs_p, 0)[:, :, FRONT:]
    kr_p = stack(outs_p, 1)[:, :, FRONT:]
    return (y_prompt, ys.reshape(bs, ts, D_MODEL), ckv_p, kr_p, stack(outs_p, 2),
            stack(outs_p, 3), stack(outs_p, 4),
            stack(outs_s, 0), stack(outs_s, 1), stack(outs_s, 2), stack(outs_s, 3),
            stack(outs_s, 4))
```

```python
import functools

import jax
import jax.numpy as jnp
import numpy as np
from jax import lax
from jax.experimental import pallas as pl
from jax.experimental.pallas import tpu as pltpu

D_MODEL = 1024
CHUNK = 64
N_META = 16
CONV_W = 256
CONV_K = 3
MLA_HEADS = 8
QK_NOPE = 64
QK_ROPE = 32
V_DIM = 64
MLA_W = MLA_HEADS * V_DIM
Q_LORA = 256
KV_LORA = 128
RWKV_HEAD = 64
RWKV_W = 256
RWKV_HEADS = 4
DECAY_LORA = 64
ICLR_LORA = 64
SHIFT_W = 3 * RWKV_W + DECAY_LORA + ICLR_LORA
ROPE_BASE = 10000.0
RMS_EPS = 1e-6
GN_EPS = 64e-5
NEG = -1e30
LOG2_E = 1.4426950408889634
CHUNK_SHIFT = CHUNK.bit_length() - 1
HEAD_SHIFT = RWKV_HEAD.bit_length() - 1
DECAY_SCALE = 0.6065306597126334

LANES = 128
QTILE = 256
KTILE = 256
FRONT = KTILE - N_META
QK_W = 2 * LANES
ONES_LANE = KV_LORA + QK_ROPE
PAD_LANE = ONES_LANE + 1
PROMPT_ROW_TILES = 8
SCAN_GROUP = 4
SAMPLE_SCAN_GROUP = 8
SCAN_T = 64
SCAN_SPLIT = 2
SCAN_UNROLL = 16
VMEM_LIMIT = 48 * 1024 * 1024
ATTN_VMEM_LIMIT = 58 * 1024 * 1024

F32 = jnp.float32
BF16 = jnp.bfloat16


def _dot(a, b):
    return jnp.dot(a, b, preferred_element_type=F32)


def _dot_nt(a, b):
    return lax.dot_general(a, b, (((1,), (1,)), ((), ())), preferred_element_type=F32)


def _rms(x, g):
    return x * lax.rsqrt(jnp.mean(x * x, axis=-1, keepdims=True) + RMS_EPS) * g


def _silu(x):
    return x * jax.nn.sigmoid(x)


def _rope_block(x, cos, sin_lo, sin_hi):
    return (x * cos + pltpu.roll(x, LANES - QK_ROPE // 2, 1) * sin_lo
            + pltpu.roll(x, QK_ROPE // 2, 1) * sin_hi)


def _proj_kernel(x_ref, g_ref, wa_ref, wq_ref, wckv_ref, wkr_ref, wgb_ref, wzc_ref, wgc_ref,
                 qg_ref, wuq_ref, bd_ref, kvg_ref, cos_ref, slo_ref, shi_ref,
                 za_ref, qp_ref, kcat_ref, ckv_ref, kr_ref, gb_ref, zc_ref, gc_ref):
    scale = (QK_NOPE + QK_ROPE) ** -0.5 * LOG2_E
    hb = _rms(x_ref[...], g_ref[...]).astype(BF16)
    za_ref[...] = _dot(hb, wa_ref[...])
    gb_ref[...] = _dot(hb, wgb_ref[...])
    zc_ref[...] = _dot(hb, wzc_ref[...])
    gc_ref[...] = _dot(hb, wgc_ref[...])

    cos, slo, shi = cos_ref[...], slo_ref[...], shi_ref[...]
    cqn = _rms(_dot(hb, wq_ref[...]), qg_ref[...]).astype(BF16)
    qe = _dot(cqn, wuq_ref[...])
    nope_w = MLA_HEADS * QK_NOPE
    qabs = _dot(qe[:, :nope_w].astype(BF16), bd_ref[...])
    lane = lax.broadcasted_iota(jnp.int32, (1, LANES), 1)
    ones_lane, pad_lane = lane == ONES_LANE - LANES, lane == PAD_LANE - LANES
    for h in range(MLA_HEADS):
        lo, hi = h * LANES, (h + 1) * LANES
        rot = _rope_block(qe[:, nope_w + lo:nope_w + hi], cos, slo, shi)
        qp_ref[h, :, 0:LANES] = (qabs[:, lo:hi] * scale).astype(BF16)
        qp_ref[h, :, LANES:QK_W] = jnp.where(pad_lane, NEG, rot * scale).astype(BF16)

    ckvn = _rms(_dot(hb, wckv_ref[...]), kvg_ref[...])
    ckv_ref[...] = ckvn
    krr = _rope_block(_dot(hb, wkr_ref[...]), cos, slo, shi)
    kr_ref[...] = krr[:, :QK_ROPE]
    kcat_ref[:, 0:LANES] = ckvn.astype(BF16)
    krr = jnp.where(ones_lane, 1.0, jnp.where(pad_lane, cos, krr))
    kcat_ref[:, LANES:QK_W] = krr.astype(BF16)


def _proj(x, lw, tabs, tm, tab_blocks):
    rows = x.shape[0]
    n = rows // tm
    row = lambda w: pl.BlockSpec((tm, w), lambda i: (i, 0))
    full = lambda a: pl.BlockSpec(a.shape, lambda i: (0,) * a.ndim)
    tab = pl.BlockSpec((tm, LANES), lambda i: (i % tab_blocks, 0))
    weights = (lw["norm_g"], lw["wa"], lw["wq"], lw["wckv"], lw["wkr"], lw["wgb"], lw["wzc"],
               lw["wgc"], lw["q_norm_g"], lw["wuq"], lw["bd"], lw["kv_norm_g"])
    out_shape = (
        jax.ShapeDtypeStruct((rows, 4 * CONV_W), F32),
        jax.ShapeDtypeStruct((MLA_HEADS, rows, QK_W), BF16),
        jax.ShapeDtypeStruct((rows, QK_W), BF16),
        jax.ShapeDtypeStruct((rows, KV_LORA), F32),
        jax.ShapeDtypeStruct((rows, QK_ROPE), F32),
        jax.ShapeDtypeStruct((rows, MLA_W), F32),
        jax.ShapeDtypeStruct((rows, SHIFT_W), F32),
        jax.ShapeDtypeStruct((rows, RWKV_W), F32),
    )
    out_specs = (row(4 * CONV_W), pl.BlockSpec((MLA_HEADS, tm, QK_W), lambda i: (0, i, 0)),
                 row(QK_W), row(KV_LORA), row(QK_ROPE), row(MLA_W), row(SHIFT_W), row(RWKV_W))
    return pl.pallas_call(
        _proj_kernel,
        grid=(n,),
        in_specs=[row(D_MODEL)] + [full(w) for w in weights] + [tab, tab, tab],
        out_specs=out_specs,
        out_shape=out_shape,
        compiler_params=pltpu.CompilerParams(dimension_semantics=("parallel",),
                                             vmem_limit_bytes=VMEM_LIMIT),
        name="proj",
    )(x, *weights, *tabs)


def _heads_out(o_lat, wv_ref, rows):
    out = None
    for h in range(MLA_HEADS):
        part = _dot(o_lat[h * rows:(h + 1) * rows].astype(BF16), wv_ref[h])
        out = part if out is None else out + part
    return out


def _attn_prompt_kernel(q_ref, k_ref, gb_ref, wv_ref, yb_ref, s_ref, mx_ref, acc_ref):
    i = pl.program_id(1)
    rows = MLA_HEADS * QTILE

    all_front = (i + 1) * QTILE <= FRONT

    @pl.when(all_front)
    def _():
        yb_ref[...] = jnp.zeros(yb_ref.shape, BF16)

    @pl.when(jnp.logical_not(all_front))
    def _():
        q = q_ref[...].reshape(rows, QK_W)
        lane = lax.broadcasted_iota(jnp.int32, (rows, LANES), 1)
        qrow = (lax.broadcasted_iota(jnp.int32, (rows, LANES), 0) & (QTILE - 1)) + i * QTILE
        last = i * (QTILE // KTILE)

        def run_tiles(start, count, fn):
            def quad(jj, carry):
                fn(start + 4 * jj, 4 * KTILE)
                return carry

            lax.fori_loop(0, count >> 2, quad, 0)
            done = start + (count & ~3)

            @pl.when((count & 2) != 0)
            def _():
                fn(done, 2 * KTILE)

            @pl.when((count & 1) != 0)
            def _():
                fn(done + (count & 2), KTILE)

        def keys(j, width):
            return k_ref[0, pl.ds(pl.multiple_of(j * KTILE, KTILE), width), :]

        def s_at(j, width):
            return s_ref.at[:, pl.ds(pl.multiple_of(j * KTILE, KTILE), width)]

        def fold(x, op):
            out = x[:, :LANES]
            for c in range(1, x.shape[1] // LANES):
                out = op(out, x[:, c * LANES:(c + 1) * LANES])
            return out

        last_visible = qrow | (CHUNK - 1)
        mx_ref[...] = jnp.full((rows, LANES), NEG, F32)

        def scores(j, width):
            s = _dot_nt(q, keys(j, width))
            blocks = []
            for c in range(width // LANES):
                key_row = lane + (j * KTILE + c * LANES)
                blocks.append(jnp.where(key_row <= last_visible,
                                        s[:, c * LANES:(c + 1) * LANES], NEG))
            s = jnp.concatenate(blocks, axis=1)
            s_at(j, width)[...] = s
            mx_ref[...] = jnp.maximum(mx_ref[...], fold(s, jnp.maximum))

        run_tiles(0, last + 1, scores)

        m = jnp.broadcast_to(jnp.max(mx_ref[...], axis=-1, keepdims=True), (rows, LANES))
        mx_ref[...] = m
        acc_ref[...] = jnp.zeros((rows, QK_W), F32)

        def weigh(j, width):
            s = s_at(j, width)[...]
            m_ = mx_ref[...]
            pb = jnp.concatenate([jnp.exp2(s[:, c * LANES:(c + 1) * LANES] - m_).astype(BF16)
                                  for c in range(width // LANES)], axis=1)
            acc_ref[...] += _dot(pb, keys(j, width))

        run_tiles(0, last + 1, weigh)

        acc = acc_ref[...]
        o_lat = acc[:, :KV_LORA] / acc[:, ONES_LANE:ONES_LANE + 1]
        yb_ref[...] = (_heads_out(o_lat, wv_ref, QTILE) * _silu(gb_ref[...])).astype(BF16)


def _attn_prompt(qp, kcat, gb, wv, batch, tp):
    nq = tp // QTILE
    rows = MLA_HEADS * QTILE
    return pl.pallas_call(
        _attn_prompt_kernel,
        grid=(batch, nq),
        in_specs=[
            pl.BlockSpec((MLA_HEADS, QTILE, QK_W), lambda b, i: (0, b * nq + i, 0)),
            pl.BlockSpec((1, tp, QK_W), lambda b, i: (b, 0, 0)),
            pl.BlockSpec((QTILE, MLA_W), lambda b, i: (b * nq + i, 0)),
            pl.BlockSpec(wv.shape, lambda b, i: (0, 0, 0)),
        ],
        out_specs=pl.BlockSpec((QTILE, MLA_W), lambda b, i: (b * nq + i, 0)),
        out_shape=jax.ShapeDtypeStruct((batch * tp, MLA_W), BF16),
        scratch_shapes=[pltpu.VMEM((rows, tp), F32),
                        pltpu.VMEM((rows, LANES), F32),
                        pltpu.VMEM((rows, QK_W), F32)],
        compiler_params=pltpu.CompilerParams(dimension_semantics=("parallel", "arbitrary"),
                                             vmem_limit_bytes=ATTN_VMEM_LIMIT),
        name="attn_prompt",
    )(qp, kcat.reshape(batch, tp, QK_W), gb, wv)


def _attn_sample_kernel(q_ref, kn_ref, ck_ref, kr_ref, gb_ref, wv_ref, yb_ref, kp_ref):
    ts = q_ref.shape[1]
    q = q_ref[...].reshape(MLA_HEADS * ts, QK_W)
    kn = kn_ref[...]
    ck = ck_ref[0, 0].astype(BF16)
    kp_ref[:, 0:KV_LORA] = ck
    kp_ref[:, KV_LORA:QK_W] = jnp.zeros((kp_ref.shape[0], QK_W - KV_LORA), BF16)
    kp_ref[:, KV_LORA:KV_LORA + QK_ROPE] = kr_ref[0, 0].astype(BF16)
    s_past = _dot_nt(q, kp_ref[...])
    s_new = _dot_nt(q, kn)
    m = jnp.maximum(jnp.max(s_past, axis=-1, keepdims=True),
                    jnp.max(s_new, axis=-1, keepdims=True))
    p_past = jnp.exp2(s_past - m)
    p_new = jnp.exp2(s_new - m)
    l = jnp.sum(p_past, axis=-1, keepdims=True) + jnp.sum(p_new, axis=-1, keepdims=True)
    o_lat = (_dot(p_past.astype(BF16), ck) + _dot(p_new.astype(BF16), kn[:, :KV_LORA])) / l
    yb_ref[...] = (_heads_out(o_lat, wv_ref, ts) * _silu(gb_ref[...])).astype(BF16)


def _attn_sample(qp, kcat, cache_ckv, cache_krope, layer, gb, wv, batch, ts):
    past = cache_ckv.shape[2]
    return pl.pallas_call(
        _attn_sample_kernel,
        grid=(batch,),
        in_specs=[
            pl.BlockSpec((MLA_HEADS, ts, QK_W), lambda b: (0, b, 0)),
            pl.BlockSpec((ts, QK_W), lambda b: (b, 0)),
            pl.BlockSpec((1, 1, past, KV_LORA), lambda b: (layer, b, 0, 0)),
            pl.BlockSpec((1, 1, past, QK_ROPE), lambda b: (layer, b, 0, 0)),
            pl.BlockSpec((ts, MLA_W), lambda b: (b, 0)),
            pl.BlockSpec(wv.shape, lambda b: (0, 0, 0)),
        ],
        out_specs=pl.BlockSpec((ts, MLA_W), lambda b: (b, 0)),
        out_shape=jax.ShapeDtypeStruct((batch * ts, MLA_W), BF16),
        scratch_shapes=[pltpu.VMEM((past, QK_W), BF16)],
        compiler_params=pltpu.CompilerParams(dimension_semantics=("parallel",),
                                             vmem_limit_bytes=VMEM_LIMIT),
        name="attn_sample",
    )(qp, kcat, cache_ckv, cache_krope, gb, wv)


def _head_sums(x, ones4):
    hi = x.astype(BF16)
    lo = (x - hi.astype(F32)).astype(BF16)
    both = _dot(jnp.concatenate([hi, lo], axis=0), ones4)
    return both[0:x.shape[0]] + both[x.shape[0]:]


def _mix_kernel(za_ref, zc_ref, gc_ref, conv0_ref, shift0_ref, wkv0_ref,
                cw_ref, mu_ref, w0_ref, w2a2_ref, a0_ref, kkw_ref, kaw_ref, bonus_ref,
                lnw_ref, lnb_ref, wtab_ref,
                ya_ref, yc_ref, conv_ref, shift_ref, wkv_ref,
                s_ref, cu_ref, cz_ref, vec_ref, vtl_ref, yt_ref, *, nblk):
    tb = pl.program_id(1)
    tblk = SCAN_T
    nstream = za_ref.shape[0]
    npair = nstream * RWKV_HEADS // 2
    hw = RWKV_HEAD

    @pl.when(tb == 0)
    def _():
        for pr in range(npair):
            b, hp = divmod(pr, 2)
            s_ref[pr, :, 0:hw] = wkv0_ref[b, 2 * hp]
            s_ref[pr, :, hw:2 * hw] = wkv0_ref[b, 2 * hp + 1]
        cu_ref[...] = conv0_ref[...]
        cz_ref[...] = shift0_ref[...]

    trow = lax.broadcasted_iota(jnp.int32, (tblk, 1), 0)
    lane128 = lax.broadcasted_iota(jnp.int32, (1, LANES), 1)
    ri = lax.broadcasted_iota(jnp.int32, (2 * LANES, 2 * LANES), 0)
    ci = lax.broadcasted_iota(jnp.int32, (2 * LANES, 2 * LANES), 1)
    ones4 = jnp.where((ri >> HEAD_SHIFT) == (ci >> HEAD_SHIFT), 1.0, 0.0).astype(BF16)
    cw = cw_ref[...]
    zpad = jnp.zeros((LANES - hw, LANES), F32)
    streams = range(nstream)
    cat = lambda parts: jnp.concatenate(parts, axis=0)
    cut = lambda x, b: x[b * tblk:(b + 1) * tblk]
    rs, ks, vs, las = [], [], [], []
    for b in streams:
        za = za_ref[b]
        xin, bg = za[:, 0:CONV_W], za[:, CONV_W:2 * CONV_W]
        cg, ga = za[:, 2 * CONV_W:3 * CONV_W], za[:, 3 * CONV_W:4 * CONV_W]
        u = cg * xin
        cu = cu_ref[b]
        u1 = jnp.where(trow == 0, cu[1:2], pltpu.roll(u, 1, 0))
        u2 = jnp.where(trow == 0, cu[0:1], jnp.where(trow == 1, cu[1:2], pltpu.roll(u, 2, 0)))
        conv = cw[0:1] * u2 + cw[1:2] * u1 + cw[2:3] * u
        ya_ref[b] = (bg * conv * _silu(ga)).astype(BF16)
        cu_ref[b] = u[tblk - (CONV_K - 1):tblk]

        zc = zc_ref[b]
        prev = jnp.where(trow == 0, cz_ref[b], pltpu.roll(zc, 1, 0))
        cz_ref[b] = zc[tblk - 1:tblk]
        zs = zc + (prev - zc) * mu_ref[...]
        r, k, v = zs[:, 0:RWKV_W], zs[:, RWKV_W:2 * RWKV_W], zs[:, 2 * RWKV_W:3 * RWKV_W]
        la = zs[:, 3 * RWKV_W:SHIFT_W]
        las.append(jnp.where(lane128 < DECAY_LORA, jnp.tanh(la), la).astype(BF16))
        rs.append(r), ks.append(k), vs.append(v)

    lora = _dot(cat(las), w2a2_ref[...])
    decays, avs, kks, kps = [], [], [], []
    for b in streams:
        dec_in = w0_ref[...] + cut(lora, b)[:, 0:RWKV_W]
        decays.append(jnp.exp(-DECAY_SCALE * jax.nn.sigmoid(dec_in)))
        a = jax.nn.sigmoid(a0_ref[...] + cut(lora, b)[:, RWKV_W:2 * RWKV_W])
        avs.append(a)
        kks.append(ks[b] * kkw_ref[...])
        kps.append(ks[b] * (1.0 + (a - 1.0) * kaw_ref[...]))
    kk_sq = _head_sums(cat([kk * kk for kk in kks]), ones4)
    rk_sum = _head_sums(cat([rs[b] * kps[b] * bonus_ref[...] for b in streams]), ones4)
    bonus_v = []
    for b in streams:
        r, v, a, kp, decay = rs[b], vs[b], avs[b], kps[b], decays[b]
        kk = kks[b] / jnp.maximum(jnp.sqrt(cut(kk_sq, b)), 1e-12)
        bonus_v.append(cut(rk_sum, b) * v)
        for n, vec in enumerate((-kk, decay, kk * a, kp, r)):
            vec_ref[n, 2 * b] = vec[:, 0:LANES]
            vec_ref[n, 2 * b + 1] = vec[:, LANES:2 * LANES]
        for hp in range(2):
            z = jnp.concatenate([v[:, hp * LANES:(hp + 1) * LANES], zpad], axis=0).T
            vt = z[0:hw] + pltpu.roll(z[hw:2 * hw], SCAN_T, 1)
            side, n = divmod(2 * b + hp, npair // 2)
            vtl_ref[n * hw:(n + 1) * hw, side * LANES:(side + 1) * LANES] = vt.astype(BF16)

    tok_of_lane = lane128 & (SCAN_T - 1)
    gsz = npair // SCAN_SPLIT
    groups = [list(range(g * gsz, (g + 1) * gsz)) for g in range(SCAN_SPLIT)]

    def lane_sums(s_list, prs, t_a, t_b):
        lhs = jnp.concatenate(
            [jnp.concatenate([(s * vec_ref[0, pr, pl.ds(t_a, 1), :]).astype(BF16),
                              (s * vec_ref[4, pr, pl.ds(t_b, 1), :]).astype(BF16)], axis=1)
             for s, pr in zip(s_list, prs)], axis=0)
        return _dot(lhs, ones4)

    def step(t, carry):
        rowv = lambda n, pr: vec_ref[n, pr, pl.ds(t, 1), :]
        vb = _dot(vtl_ref[...], wtab_ref[t])
        hit = tok_of_lane == t - 1
        for prs in groups:
            s_old = [s_ref[pr] for pr in prs]
            sums = lane_sums(s_old, prs, t, jnp.maximum(t - 1, 0))
            for n, pr in enumerate(prs):
                rows = slice(n * hw, (n + 1) * hw)
                side, m = divmod(pr, npair // 2)
                vbp = vb[m * hw:(m + 1) * hw, side * LANES:(side + 1) * LANES]
                s_ref[pr] = (s_old[n] * rowv(1, pr) + sums[rows, 0:LANES] * rowv(2, pr)
                             + vbp * rowv(3, pr))
                yt_ref[pr] = jnp.where(hit, sums[rows, LANES:2 * LANES], yt_ref[pr])
        return carry

    yt_ref[...] = jnp.zeros(yt_ref.shape, F32)
    lax.fori_loop(0, SCAN_T, step, 0, unroll=SCAN_UNROLL)
    hit = tok_of_lane == SCAN_T - 1
    for prs in groups:
        sums = lane_sums([s_ref[pr] for pr in prs], prs, SCAN_T - 1, SCAN_T - 1)
        for n, pr in enumerate(prs):
            yt_ref[pr] = jnp.where(hit, sums[n * hw:(n + 1) * hw, LANES:2 * LANES], yt_ref[pr])

    ys = []
    for b in streams:
        halves = []
        for hp in range(2):
            z = jnp.concatenate([yt_ref[2 * b + hp], zpad], axis=0).T
            halves.append(z[0:SCAN_T] + pltpu.roll(z[SCAN_T:2 * SCAN_T], hw, 1))
        ys.append(jnp.concatenate(halves, axis=1))
    ys = cat(ys)
    d = ys - _head_sums(ys, ones4) * (1.0 / RWKV_HEAD)
    var = _head_sums(d * d, ones4) * (1.0 / RWKV_HEAD)
    yn = d * lax.rsqrt(var + GN_EPS) * lnw_ref[...] + lnb_ref[...]
    for b in streams:
        yc_ref[b] = ((cut(yn, b) + bonus_v[b]) * _silu(gc_ref[b])).astype(BF16)

    @pl.when(tb == nblk - 1)
    def _():
        conv_ref[...] = cu_ref[...]
        shift_ref[...] = cz_ref[...]
        for pr in range(npair):
            b, hp = divmod(pr, 2)
            wkv_ref[b, 2 * hp] = s_ref[pr, :, 0:hw]
            wkv_ref[b, 2 * hp + 1] = s_ref[pr, :, hw:2 * hw]


def _mix(za, zc, gc, conv0, shift0, wkv0, lw, batch, t, nstream):
    groups = batch // nstream
    nblk = t // SCAN_T
    npair = nstream * RWKV_HEADS // 2
    seq = lambda w: pl.BlockSpec((nstream, SCAN_T, w), lambda g, i: (g, i, 0))
    state3 = lambda a: pl.BlockSpec((nstream,) + a.shape[1:], lambda g, i: (g, 0, 0))
    state4 = pl.BlockSpec((nstream, RWKV_HEADS, RWKV_HEAD, RWKV_HEAD),
                          lambda g, i: (g, 0, 0, 0))
    full = lambda a: pl.BlockSpec(a.shape, lambda g, i: (0,) * a.ndim)
    r_, c_ = np.arange(2 * LANES)[:, None], np.arange(2 * LANES)[None, :]
    wtab = jnp.asarray(((r_ & (SCAN_T - 1)) == np.arange(SCAN_T)[:, None, None])
                       & ((r_ >> HEAD_SHIFT) == (c_ >> HEAD_SHIFT)), dtype=BF16)
    params = (lw["conv_w"], lw["shift_mu"], lw["decay_w0"], lw["w2a2"], lw["iclr_a0"],
              lw["key_kk"], lw["key_ka"], lw["bonus_rk"], lw["lnx_w"], lw["lnx_b"], wtab)
    out_shape = (
        jax.ShapeDtypeStruct((batch, t, CONV_W), BF16),
        jax.ShapeDtypeStruct((batch, t, RWKV_W), BF16),
        jax.ShapeDtypeStruct(conv0.shape, F32),
        jax.ShapeDtypeStruct(shift0.shape, F32),
        jax.ShapeDtypeStruct(wkv0.shape, F32),
    )
    return pl.pallas_call(
        functools.partial(_mix_kernel, nblk=nblk),
        grid=(groups, nblk),
        in_specs=[seq(4 * CONV_W), seq(SHIFT_W), seq(RWKV_W), state3(conv0), state3(shift0),
                  state4] + [full(p) for p in params],
        out_specs=(seq(CONV_W), seq(RWKV_W), state3(conv0), state3(shift0), state4),
        out_shape=out_shape,
        scratch_shapes=[
            pltpu.VMEM((npair, RWKV_HEAD, LANES), F32),
            pltpu.VMEM((nstream, CONV_K - 1, CONV_W), F32),
            pltpu.VMEM((nstream, 1, SHIFT_W), F32),
            pltpu.VMEM((5, npair, SCAN_T, LANES), F32),
            pltpu.VMEM((npair // 2 * RWKV_HEAD, 2 * LANES), BF16),
            pltpu.VMEM((npair, RWKV_HEAD, LANES), F32),
        ],
        compiler_params=pltpu.CompilerParams(dimension_semantics=("parallel", "arbitrary"),
                                             vmem_limit_bytes=VMEM_LIMIT),
        name="mix",
    )(za, zc, gc, conv0, shift0, wkv0, *params)


def _out_kernel(ya_ref, yb_ref, yc_ref, x_ref, wa_ref, wb_ref, wc_ref, fg_ref, o_ref, *, final):
    y = (_dot(ya_ref[...], wa_ref[...]) + _dot(yb_ref[...], wb_ref[...])
         + _dot(yc_ref[...], wc_ref[...]))
    y = x_ref[...] + y
    if final:
        y = _rms(y, fg_ref[...])
    o_ref[...] = y


def _out(ya, yb, yc, x, lw, final_g, final, tm, t=None, skip=0):
    rows = x.shape[0]
    if skip:
        assert skip % tm == 0 and t % tm == 0
        n_skip, n_keep = skip // tm, (t - skip) // tm
        src = lambda i: (i + (i // n_keep + 1) * n_skip, 0)
        out_rows = rows // t * (t - skip)
    else:
        src = lambda i: (i, 0)
        out_rows = rows
    row = lambda w: pl.BlockSpec((tm, w), src)
    full = lambda a: pl.BlockSpec(a.shape, lambda i: (0,) * a.ndim)
    weights = (lw["wo_a"], lw["wo_b"], lw["wo_c"], final_g)
    return pl.pallas_call(
        functools.partial(_out_kernel, final=final),
        grid=(out_rows // tm,),
        in_specs=[row(CONV_W), row(MLA_W), row(RWKV_W), row(D_MODEL)] + [full(w) for w in weights],
        out_specs=pl.BlockSpec((tm, D_MODEL), lambda i: (i, 0)),
        out_shape=jax.ShapeDtypeStruct((out_rows, D_MODEL), F32),
        compiler_params=pltpu.CompilerParams(dimension_semantics=("parallel",),
                                             vmem_limit_bytes=VMEM_LIMIT),
        name="out",
    )(ya, yb, yc, x, *weights)


def _layer_weights(l, norm_g, w_in, conv_w, q_norm_g, w_uq, kv_norm_g, w_ukv, shift_mu,
                   decay_w0, decay_w2, iclr_a0, iclr_a2, key_kk, key_ka, bonus_rk, lnx_w, lnx_b,
                   w_out):
    row = lambda p: p[l].reshape(1, -1).astype(F32)
    wi = w_in[l].astype(BF16)
    o = np.cumsum([0, 4 * CONV_W, Q_LORA, KV_LORA, QK_ROPE, MLA_W, SHIFT_W, RWKV_W]).tolist()
    cols = lambda j: wi[:, o[j]:o[j + 1]]
    uq = w_uq[l].astype(BF16).reshape(Q_LORA, MLA_HEADS, QK_NOPE + QK_ROPE)
    uq_rope = jnp.pad(uq[:, :, QK_NOPE:], ((0, 0), (0, 0), (0, LANES - QK_ROPE)))
    ukv = w_ukv[l].astype(BF16).reshape(KV_LORA, MLA_HEADS, QK_NOPE + V_DIM)
    eye = jnp.eye(MLA_HEADS, dtype=BF16)
    bd = jnp.einsum("chd,hg->hdgc", ukv[:, :, :QK_NOPE], eye).reshape(
        MLA_HEADS * QK_NOPE, MLA_HEADS * KV_LORA)
    wv = jnp.einsum("che,hg->hcge", ukv[:, :, QK_NOPE:], eye).reshape(
        MLA_HEADS, KV_LORA, MLA_W)
    zeros = jnp.zeros((DECAY_LORA, RWKV_W), BF16)
    w2a2 = jnp.concatenate([
        jnp.concatenate([decay_w2[l].astype(BF16), zeros], axis=1),
        jnp.concatenate([zeros, iclr_a2[l].astype(BF16)], axis=1)], axis=0)
    wo = w_out[l].astype(BF16)
    return {
        "norm_g": row(norm_g), "wa": cols(0), "wq": cols(1), "wckv": cols(2),
        "wkr": jnp.pad(cols(3), ((0, 0), (0, LANES - QK_ROPE))),
        "wgb": cols(4), "wzc": cols(5), "wgc": cols(6),
        "q_norm_g": row(q_norm_g),
        "wuq": jnp.concatenate([uq[:, :, :QK_NOPE].reshape(Q_LORA, -1),
                                uq_rope.reshape(Q_LORA, -1)], axis=1),
        "bd": bd, "wv": wv, "kv_norm_g": row(kv_norm_g),
        "conv_w": conv_w[l].astype(F32), "shift_mu": row(shift_mu), "decay_w0": row(decay_w0),
        "w2a2": w2a2, "iclr_a0": row(iclr_a0), "key_kk": row(key_kk), "key_ka": row(key_ka),
        "bonus_rk": row(bonus_rk), "lnx_w": row(lnx_w), "lnx_b": row(lnx_b),
        "wo_a": wo[0:CONV_W], "wo_b": wo[CONV_W:CONV_W + MLA_W], "wo_c": wo[CONV_W + MLA_W:],
    }


def _rope_tables(pos, front):
    half = QK_ROPE // 2
    n = pos.shape[0]
    inv = ROPE_BASE ** (-jnp.arange(half, dtype=F32) * 2.0 / QK_ROPE)
    ang = pos.astype(F32)[:, None] * inv[None, :]
    cos, sin = jnp.cos(ang), jnp.sin(ang)
    z = jnp.zeros((n, LANES - QK_ROPE), F32)
    zh = jnp.zeros((n, half), F32)
    flag = (jnp.arange(n) < front).astype(F32)[:, None]
    pad_col = PAD_LANE - LANES - QK_ROPE
    zflag = jnp.concatenate([z[:, :pad_col], flag, z[:, pad_col + 1:]], axis=1)
    return (jnp.concatenate([cos, cos, zflag], axis=1),
            jnp.concatenate([-sin, zh, z], axis=1),
            jnp.concatenate([zh, sin, z], axis=1))


def _stream(x, batch, t, tm, tabs, tab_blocks, states, attn, lws, final_g, nstream, skip=0):
    outs = []
    depth = len(lws)
    for l, lw in enumerate(lws):
        za, qp, kcat, ckv, kr, gb, zc, gc = _proj(x, lw, tabs, tm, tab_blocks)
        yb = attn(l, lw, qp, kcat, gb)
        conv0, shift0, wkv0 = states[l]
        ya, yc, conv, shift, wkv = _mix(
            za.reshape(batch, t, -1), zc.reshape(batch, t, -1), gc.reshape(batch, t, -1),
            conv0, shift0, wkv0, lw, batch, t, nstream)
        last = l == depth - 1
        x = _out(ya.reshape(batch * t, -1), yb, yc.reshape(batch * t, -1), x, lw, final_g, last,
                 skip if last and skip else tm, t, skip if last else 0)
        outs.append((ckv.reshape(batch, t, -1), kr.reshape(batch, t, -1), conv,
                     shift.reshape(batch, -1), wkv))
    return x, outs


def kernel(x_prompt, x_sample, cache_ckv, cache_krope, state_conv, state_shift, state_wkv,
           meta_tokens, norm_g, w_in, conv_w, q_norm_g, w_uq, kv_norm_g, w_ukv, shift_mu,
           decay_w0, decay_w2, iclr_a0, iclr_a2, key_kk, key_ka, bonus_rk, lnx_w, lnx_b,
           w_out, final_g):
    depth = w_in.shape[0]
    bp, seq = x_prompt.shape[:2]
    bs, ts = x_sample.shape[:2]
    past = cache_ckv.shape[2]
    tp = FRONT + N_META + seq
    assert tp % KTILE == 0 and bp % SCAN_GROUP == 0 and bs % SAMPLE_SCAN_GROUP == 0
    assert seq % CHUNK == 0 and ts == SCAN_T and tp % SCAN_T == 0
    lws = [_layer_weights(l, norm_g, w_in, conv_w, q_norm_g, w_uq, kv_norm_g, w_ukv, shift_mu,
                          decay_w0, decay_w2, iclr_a0, iclr_a2, key_kk, key_ka, bonus_rk,
                          lnx_w, lnx_b, w_out) for l in range(depth)]
    fg = final_g.reshape(1, -1).astype(F32)

    xp = jnp.concatenate([
        jnp.zeros((bp, FRONT, D_MODEL), F32),
        jnp.broadcast_to(meta_tokens.astype(F32)[None], (bp, N_META, D_MODEL)),
        x_prompt], axis=1).reshape(bp * tp, D_MODEL)
    tm_p = tp // PROMPT_ROW_TILES
    tabs_p = _rope_tables(jnp.maximum(jnp.arange(tp, dtype=jnp.int32) - FRONT, 0), FRONT)
    zero_states = [(jnp.zeros((bp, CONV_K - 1, CONV_W), F32), jnp.zeros((bp, 1, SHIFT_W), F32),
                    jnp.zeros((bp, RWKV_HEADS, RWKV_HEAD, RWKV_HEAD), F32))] * depth
    attn_p = lambda l, lw, qp, kcat, gb: _attn_prompt(qp, kcat, gb, lw["wv"], bp, tp)
    yp, outs_p = _stream(xp, bp, tp, tm_p, tabs_p, tp // tm_p, zero_states, attn_p, lws, fg,
                         SCAN_GROUP, skip=FRONT + N_META)

    tm_s = 512
    reps = tm_s // ts
    tabs_s = tuple(jnp.tile(tb_, (reps, 1))
                   for tb_ in _rope_tables(past + jnp.arange(ts, dtype=jnp.int32), 0))
    states_s = [(state_conv[l], state_shift[l].reshape(bs, 1, SHIFT_W), state_wkv[l])
                for l in range(depth)]
    attn_s = lambda l, lw, qp, kcat, gb: _attn_sample(qp, kcat, cache_ckv, cache_krope, l, gb,
                                                      lw["wv"], bs, ts)
    ys, outs_s = _stream(x_sample.reshape(bs * ts, D_MODEL), bs, ts, tm_s, tabs_s, 1,
                         states_s, attn_s, lws, fg, SAMPLE_SCAN_GROUP)

    stack = lambda outs, j: jnp.stack([o[j] for o in outs])
    y_prompt = yp.reshape(bp, seq, D_MODEL)
    ckv_p = stack(outs_p, 0)[:, :, FRONT:]
    kr_p = stack(outs_p, 1)[:, :, FRONT:]
    return (y_prompt, ys.reshape(bs, ts, D_MODEL), ckv_p, kr_p, stack(outs_p, 2),
            stack(outs_p, 3), stack(outs_p, 4),
            stack(outs_s, 0), stack(outs_s, 1), stack(outs_s, 2), stack(outs_s, 3),
            stack(outs_s, 4))
```

```python
import functools

import jax
import jax.numpy as jnp
import numpy as np
from jax import lax
from jax.experimental import pallas as pl
from jax.experimental.pallas import tpu as pltpu

D_MODEL = 1024
CHUNK = 64
N_META = 16
CONV_W = 256
CONV_K = 3
MLA_HEADS = 8
QK_NOPE = 64
QK_ROPE = 32
V_DIM = 64
MLA_W = MLA_HEADS * V_DIM
Q_LORA = 256
KV_LORA = 128
RWKV_HEAD = 64
RWKV_W = 256
RWKV_HEADS = 4
DECAY_LORA = 64
ICLR_LORA = 64
SHIFT_W = 3 * RWKV_W + DECAY_LORA + ICLR_LORA
ROPE_BASE = 10000.0
RMS_EPS = 1e-6
GN_EPS = 64e-5
NEG = -1e30
LOG2_E = 1.4426950408889634
CHUNK_SHIFT = CHUNK.bit_length() - 1
HEAD_SHIFT = RWKV_HEAD.bit_length() - 1
DECAY_SCALE = 0.6065306597126334

LANES = 128
QTILE = 256
KTILE = 256
FRONT = KTILE - N_META
QK_W = 2 * LANES
ONES_LANE = KV_LORA + QK_ROPE
PAD_LANE = ONES_LANE + 1
PROMPT_ROW_TILES = 8
SCAN_GROUP = 4
SAMPLE_SCAN_GROUP = 8
SCAN_T = 64
SCAN_SPLIT = 2
SCAN_UNROLL = 32
VMEM_LIMIT = 48 * 1024 * 1024
ATTN_VMEM_LIMIT = 58 * 1024 * 1024

F32 = jnp.float32
BF16 = jnp.bfloat16


def _dot(a, b):
    return jnp.dot(a, b, preferred_element_type=F32)


def _dot_nt(a, b):
    return lax.dot_general(a, b, (((1,), (1,)), ((), ())), preferred_element_type=F32)


def _rms(x, g):
    return x * lax.rsqrt(jnp.mean(x * x, axis=-1, keepdims=True) + RMS_EPS) * g


def _silu(x):
    return x * jax.nn.sigmoid(x)


def _rope_block(x, cos, sin_lo, sin_hi):
    return (x * cos + pltpu.roll(x, LANES - QK_ROPE // 2, 1) * sin_lo
            + pltpu.roll(x, QK_ROPE // 2, 1) * sin_hi)


def _proj_kernel(x_ref, g_ref, wa_ref, wq_ref, wckv_ref, wkr_ref, wgb_ref, wzc_ref, wgc_ref,
                 qg_ref, wuq_ref, bd_ref, kvg_ref, cos_ref, slo_ref, shi_ref,
                 za_ref, qp_ref, kcat_ref, ckv_ref, kr_ref, gb_ref, zc_ref, gc_ref):
    scale = (QK_NOPE + QK_ROPE) ** -0.5 * LOG2_E
    hb = _rms(x_ref[...], g_ref[...]).astype(BF16)
    za_ref[...] = _dot(hb, wa_ref[...])
    gb_ref[...] = _dot(hb, wgb_ref[...])
    zc_ref[...] = _dot(hb, wzc_ref[...])
    gc_ref[...] = _dot(hb, wgc_ref[...])

    cos, slo, shi = cos_ref[...], slo_ref[...], shi_ref[...]
    cqn = _rms(_dot(hb, wq_ref[...]), qg_ref[...]).astype(BF16)
    qe = _dot(cqn, wuq_ref[...])
    nope_w = MLA_HEADS * QK_NOPE
    qabs = _dot(qe[:, :nope_w].astype(BF16), bd_ref[...])
    lane = lax.broadcasted_iota(jnp.int32, (1, LANES), 1)
    ones_lane, pad_lane = lane == ONES_LANE - LANES, lane == PAD_LANE - LANES
    for h in range(MLA_HEADS):
        lo, hi = h * LANES, (h + 1) * LANES
        rot = _rope_block(qe[:, nope_w + lo:nope_w + hi], cos, slo, shi)
        qp_ref[h, :, 0:LANES] = (qabs[:, lo:hi] * scale).astype(BF16)
        qp_ref[h, :, LANES:QK_W] = jnp.where(pad_lane, NEG, rot * scale).astype(BF16)

    ckvn = _rms(_dot(hb, wckv_ref[...]), kvg_ref[...])
    ckv_ref[...] = ckvn
    krr = _rope_block(_dot(hb, wkr_ref[...]), cos, slo, shi)
    kr_ref[...] = krr[:, :QK_ROPE]
    kcat_ref[:, 0:LANES] = ckvn.astype(BF16)
    krr = jnp.where(ones_lane, 1.0, jnp.where(pad_lane, cos, krr))
    kcat_ref[:, LANES:QK_W] = krr.astype(BF16)


def _proj(x, lw, tabs, tm, tab_blocks):
    rows = x.shape[0]
    n = rows // tm
    row = lambda w: pl.BlockSpec((tm, w), lambda i: (i, 0))
    full = lambda a: pl.BlockSpec(a.shape, lambda i: (0,) * a.ndim)
    tab = pl.BlockSpec((tm, LANES), lambda i: (i % tab_blocks, 0))
    weights = (lw["norm_g"], lw["wa"], lw["wq"], lw["wckv"], lw["wkr"], lw["wgb"], lw["wzc"],
               lw["wgc"], lw["q_norm_g"], lw["wuq"], lw["bd"], lw["kv_norm_g"])
    out_shape = (
        jax.ShapeDtypeStruct((rows, 4 * CONV_W), F32),
        jax.ShapeDtypeStruct((MLA_HEADS, rows, QK_W), BF16),
        jax.ShapeDtypeStruct((rows, QK_W), BF16),
        jax.ShapeDtypeStruct((rows, KV_LORA), F32),
        jax.ShapeDtypeStruct((rows, QK_ROPE), F32),
        jax.ShapeDtypeStruct((rows, MLA_W), F32),
        jax.ShapeDtypeStruct((rows, SHIFT_W), F32),
        jax.ShapeDtypeStruct((rows, RWKV_W), F32),
    )
    out_specs = (row(4 * CONV_W), pl.BlockSpec((MLA_HEADS, tm, QK_W), lambda i: (0, i, 0)),
                 row(QK_W), row(KV_LORA), row(QK_ROPE), row(MLA_W), row(SHIFT_W), row(RWKV_W))
    return pl.pallas_call(
        _proj_kernel,
        grid=(n,),
        in_specs=[row(D_MODEL)] + [full(w) for w in weights] + [tab, tab, tab],
        out_specs=out_specs,
        out_shape=out_shape,
        compiler_params=pltpu.CompilerParams(dimension_semantics=("parallel",),
                                             vmem_limit_bytes=VMEM_LIMIT),
        name="proj",
    )(x, *weights, *tabs)


def _heads_out(o_lat, wv_ref, rows):
    out = None
    for h in range(MLA_HEADS):
        part = _dot(o_lat[h * rows:(h + 1) * rows].astype(BF16), wv_ref[h])
        out = part if out is None else out + part
    return out


def _attn_prompt_kernel(q_ref, k_ref, gb_ref, wv_ref, yb_ref, s_ref, mx_ref, acc_ref):
    i = pl.program_id(1)
    rows = MLA_HEADS * QTILE

    all_front = (i + 1) * QTILE <= FRONT

    @pl.when(all_front)
    def _():
        yb_ref[...] = jnp.zeros(yb_ref.shape, BF16)

    @pl.when(jnp.logical_not(all_front))
    def _():
        q = q_ref[...].reshape(rows, QK_W)
        lane = lax.broadcasted_iota(jnp.int32, (rows, LANES), 1)
        qrow = (lax.broadcasted_iota(jnp.int32, (rows, LANES), 0) & (QTILE - 1)) + i * QTILE
        last = i * (QTILE // KTILE)

        def run_tiles(start, count, fn):
            def quad(jj, carry):
                fn(start + 4 * jj, 4 * KTILE)
                return carry

            lax.fori_loop(0, count >> 2, quad, 0)
            done = start + (count & ~3)

            @pl.when((count & 2) != 0)
            def _():
                fn(done, 2 * KTILE)

            @pl.when((count & 1) != 0)
            def _():
                fn(done + (count & 2), KTILE)

        def keys(j, width):
            return k_ref[0, pl.ds(pl.multiple_of(j * KTILE, KTILE), width), :]

        def s_at(j, width):
            return s_ref.at[:, pl.ds(pl.multiple_of(j * KTILE, KTILE), width)]

        def fold(x, op):
            out = x[:, :LANES]
            for c in range(1, x.shape[1] // LANES):
                out = op(out, x[:, c * LANES:(c + 1) * LANES])
            return out

        last_visible = qrow | (CHUNK - 1)
        mx_ref[...] = jnp.full((rows, LANES), NEG, F32)

        def scores(j, width):
            s = _dot_nt(q, keys(j, width))
            blocks = []
            for c in range(width // LANES):
                key_row = lane + (j * KTILE + c * LANES)
                blocks.append(jnp.where(key_row <= last_visible,
                                        s[:, c * LANES:(c + 1) * LANES], NEG))
            s = jnp.concatenate(blocks, axis=1)
            s_at(j, width)[...] = s
            mx_ref[...] = jnp.maximum(mx_ref[...], fold(s, jnp.maximum))

        run_tiles(0, last + 1, scores)

        m = jnp.broadcast_to(jnp.max(mx_ref[...], axis=-1, keepdims=True), (rows, LANES))
        mx_ref[...] = m
        acc_ref[...] = jnp.zeros((rows, QK_W), F32)

        def weigh(j, width):
            s = s_at(j, width)[...]
            m_ = mx_ref[...]
            pb = jnp.concatenate([jnp.exp2(s[:, c * LANES:(c + 1) * LANES] - m_).astype(BF16)
                                  for c in range(width // LANES)], axis=1)
            acc_ref[...] += _dot(pb, keys(j, width))

        run_tiles(0, last + 1, weigh)

        acc = acc_ref[...]
        o_lat = acc[:, :KV_LORA] / acc[:, ONES_LANE:ONES_LANE + 1]
        yb_ref[...] = (_heads_out(o_lat, wv_ref, QTILE) * _silu(gb_ref[...])).astype(BF16)


def _attn_prompt(qp, kcat, gb, wv, batch, tp):
    nq = tp // QTILE
    rows = MLA_HEADS * QTILE
    return pl.pallas_call(
        _attn_prompt_kernel,
        grid=(batch, nq),
        in_specs=[
            pl.BlockSpec((MLA_HEADS, QTILE, QK_W), lambda b, i: (0, b * nq + i, 0)),
            pl.BlockSpec((1, tp, QK_W), lambda b, i: (b, 0, 0)),
            pl.BlockSpec((QTILE, MLA_W), lambda b, i: (b * nq + i, 0)),
            pl.BlockSpec(wv.shape, lambda b, i: (0, 0, 0)),
        ],
        out_specs=pl.BlockSpec((QTILE, MLA_W), lambda b, i: (b * nq + i, 0)),
        out_shape=jax.ShapeDtypeStruct((batch * tp, MLA_W), BF16),
        scratch_shapes=[pltpu.VMEM((rows, tp), F32),
                        pltpu.VMEM((rows, LANES), F32),
                        pltpu.VMEM((rows, QK_W), F32)],
        compiler_params=pltpu.CompilerParams(dimension_semantics=("parallel", "arbitrary"),
                                             vmem_limit_bytes=ATTN_VMEM_LIMIT),
        name="attn_prompt",
    )(qp, kcat.reshape(batch, tp, QK_W), gb, wv)


def _attn_sample_kernel(q_ref, kn_ref, ck_ref, kr_ref, gb_ref, wv_ref, yb_ref, kp_ref):
    ts = q_ref.shape[1]
    q = q_ref[...].reshape(MLA_HEADS * ts, QK_W)
    kn = kn_ref[...]
    ck = ck_ref[0, 0].astype(BF16)
    kp_ref[:, 0:KV_LORA] = ck
    kp_ref[:, KV_LORA:QK_W] = jnp.zeros((kp_ref.shape[0], QK_W - KV_LORA), BF16)
    kp_ref[:, KV_LORA:KV_LORA + QK_ROPE] = kr_ref[0, 0].astype(BF16)
    s_past = _dot_nt(q, kp_ref[...])
    s_new = _dot_nt(q, kn)
    m = jnp.maximum(jnp.max(s_past, axis=-1, keepdims=True),
                    jnp.max(s_new, axis=-1, keepdims=True))
    p_past = jnp.exp2(s_past - m)
    p_new = jnp.exp2(s_new - m)
    l = jnp.sum(p_past, axis=-1, keepdims=True) + jnp.sum(p_new, axis=-1, keepdims=True)
    o_lat = (_dot(p_past.astype(BF16), ck) + _dot(p_new.astype(BF16), kn[:, :KV_LORA])) / l
    yb_ref[...] = (_heads_out(o_lat, wv_ref, ts) * _silu(gb_ref[...])).astype(BF16)


def _attn_sample(qp, kcat, cache_ckv, cache_krope, layer, gb, wv, batch, ts):
    past = cache_ckv.shape[2]
    return pl.pallas_call(
        _attn_sample_kernel,
        grid=(batch,),
        in_specs=[
            pl.BlockSpec((MLA_HEADS, ts, QK_W), lambda b: (0, b, 0)),
            pl.BlockSpec((ts, QK_W), lambda b: (b, 0)),
            pl.BlockSpec((1, 1, past, KV_LORA), lambda b: (layer, b, 0, 0)),
            pl.BlockSpec((1, 1, past, QK_ROPE), lambda b: (layer, b, 0, 0)),
            pl.BlockSpec((ts, MLA_W), lambda b: (b, 0)),
            pl.BlockSpec(wv.shape, lambda b: (0, 0, 0)),
        ],
        out_specs=pl.BlockSpec((ts, MLA_W), lambda b: (b, 0)),
        out_shape=jax.ShapeDtypeStruct((batch * ts, MLA_W), BF16),
        scratch_shapes=[pltpu.VMEM((past, QK_W), BF16)],
        compiler_params=pltpu.CompilerParams(dimension_semantics=("parallel",),
                                             vmem_limit_bytes=VMEM_LIMIT),
        name="attn_sample",
    )(qp, kcat, cache_ckv, cache_krope, gb, wv)


def _head_sums(x, ones4):
    hi = x.astype(BF16)
    lo = (x - hi.astype(F32)).astype(BF16)
    both = _dot(jnp.concatenate([hi, lo], axis=0), ones4)
    return both[0:x.shape[0]] + both[x.shape[0]:]


def _mix_kernel(za_ref, zc_ref, gc_ref, conv0_ref, shift0_ref, wkv0_ref,
                cw_ref, mu_ref, w0_ref, w2a2_ref, a0_ref, kkw_ref, kaw_ref, bonus_ref,
                lnw_ref, lnb_ref, wtab_ref,
                ya_ref, yc_ref, conv_ref, shift_ref, wkv_ref,
                s_ref, cu_ref, cz_ref, vec_ref, vtl_ref, yt_ref, *, nblk):
    tb = pl.program_id(1)
    tblk = SCAN_T
    nstream = za_ref.shape[0]
    npair = nstream * RWKV_HEADS // 2
    hw = RWKV_HEAD

    @pl.when(tb == 0)
    def _():
        for pr in range(npair):
            b, hp = divmod(pr, 2)
            s_ref[pr, :, 0:hw] = wkv0_ref[b, 2 * hp]
            s_ref[pr, :, hw:2 * hw] = wkv0_ref[b, 2 * hp + 1]
        cu_ref[...] = conv0_ref[...]
        cz_ref[...] = shift0_ref[...]

    trow = lax.broadcasted_iota(jnp.int32, (tblk, 1), 0)
    lane128 = lax.broadcasted_iota(jnp.int32, (1, LANES), 1)
    ri = lax.broadcasted_iota(jnp.int32, (2 * LANES, 2 * LANES), 0)
    ci = lax.broadcasted_iota(jnp.int32, (2 * LANES, 2 * LANES), 1)
    ones4 = jnp.where((ri >> HEAD_SHIFT) == (ci >> HEAD_SHIFT), 1.0, 0.0).astype(BF16)
    cw = cw_ref[...]
    zpad = jnp.zeros((LANES - hw, LANES), F32)
    streams = range(nstream)
    cat = lambda parts: jnp.concatenate(parts, axis=0)
    cut = lambda x, b: x[b * tblk:(b + 1) * tblk]
    rs, ks, vs, las = [], [], [], []
    for b in streams:
        za = za_ref[b]
        xin, bg = za[:, 0:CONV_W], za[:, CONV_W:2 * CONV_W]
        cg, ga = za[:, 2 * CONV_W:3 * CONV_W], za[:, 3 * CONV_W:4 * CONV_W]
        u = cg * xin
        cu = cu_ref[b]
        u1 = jnp.where(trow == 0, cu[1:2], pltpu.roll(u, 1, 0))
        u2 = jnp.where(trow == 0, cu[0:1], jnp.where(trow == 1, cu[1:2], pltpu.roll(u, 2, 0)))
        conv = cw[0:1] * u2 + cw[1:2] * u1 + cw[2:3] * u
        ya_ref[b] = (bg * conv * _silu(ga)).astype(BF16)
        cu_ref[b] = u[tblk - (CONV_K - 1):tblk]

        zc = zc_ref[b]
        prev = jnp.where(trow == 0, cz_ref[b], pltpu.roll(zc, 1, 0))
        cz_ref[b] = zc[tblk - 1:tblk]
        zs = zc + (prev - zc) * mu_ref[...]
        r, k, v = zs[:, 0:RWKV_W], zs[:, RWKV_W:2 * RWKV_W], zs[:, 2 * RWKV_W:3 * RWKV_W]
        la = zs[:, 3 * RWKV_W:SHIFT_W]
        las.append(jnp.where(lane128 < DECAY_LORA, jnp.tanh(la), la).astype(BF16))
        rs.append(r), ks.append(k), vs.append(v)

    lora = _dot(cat(las), w2a2_ref[...])
    decays, avs, kks, kps = [], [], [], []
    for b in streams:
        dec_in = w0_ref[...] + cut(lora, b)[:, 0:RWKV_W]
        decays.append(jnp.exp(-DECAY_SCALE * jax.nn.sigmoid(dec_in)))
        a = jax.nn.sigmoid(a0_ref[...] + cut(lora, b)[:, RWKV_W:2 * RWKV_W])
        avs.append(a)
        kks.append(ks[b] * kkw_ref[...])
        kps.append(ks[b] * (1.0 + (a - 1.0) * kaw_ref[...]))
    kk_sq = _head_sums(cat([kk * kk for kk in kks]), ones4)
    rk_sum = _head_sums(cat([rs[b] * kps[b] * bonus_ref[...] for b in streams]), ones4)
    bonus_v = []
    for b in streams:
        r, v, a, kp, decay = rs[b], vs[b], avs[b], kps[b], decays[b]
        kk = kks[b] / jnp.maximum(jnp.sqrt(cut(kk_sq, b)), 1e-12)
        bonus_v.append(cut(rk_sum, b) * v)
        for n, vec in enumerate((-kk, decay, kk * a, kp, r)):
            vec_ref[n, 2 * b] = vec[:, 0:LANES]
            vec_ref[n, 2 * b + 1] = vec[:, LANES:2 * LANES]
        for hp in range(2):
            z = jnp.concatenate([v[:, hp * LANES:(hp + 1) * LANES], zpad], axis=0).T
            vt = z[0:hw] + pltpu.roll(z[hw:2 * hw], SCAN_T, 1)
            side, n = divmod(2 * b + hp, npair // 2)
            vtl_ref[n * hw:(n + 1) * hw, side * LANES:(side + 1) * LANES] = vt.astype(BF16)

    tok_of_lane = lane128 & (SCAN_T - 1)
    gsz = npair // SCAN_SPLIT
    groups = [list(range(g * gsz, (g + 1) * gsz)) for g in range(SCAN_SPLIT)]

    def lane_sums(s_list, prs, t_a, t_b):
        lhs = jnp.concatenate(
            [jnp.concatenate([(s * vec_ref[0, pr, pl.ds(t_a, 1), :]).astype(BF16),
                              (s * vec_ref[4, pr, pl.ds(t_b, 1), :]).astype(BF16)], axis=1)
             for s, pr in zip(s_list, prs)], axis=0)
        return _dot(lhs, ones4)

    def step(t, carry):
        rowv = lambda n, pr: vec_ref[n, pr, pl.ds(t, 1), :]
        vb = _dot(vtl_ref[...], wtab_ref[t])
        hit = tok_of_lane == t - 1
        for prs in groups:
            s_old = [s_ref[pr] for pr in prs]
            sums = lane_sums(s_old, prs, t, jnp.maximum(t - 1, 0))
            for n, pr in enumerate(prs):
                rows = slice(n * hw, (n + 1) * hw)
                side, m = divmod(pr, npair // 2)
                vbp = vb[m * hw:(m + 1) * hw, side * LANES:(side + 1) * LANES]
                s_ref[pr] = (s_old[n] * rowv(1, pr) + sums[rows, 0:LANES] * rowv(2, pr)
                             + vbp * rowv(3, pr))
                yt_ref[pr] = jnp.where(hit, sums[rows, LANES:2 * LANES], yt_ref[pr])
        return carry

    yt_ref[...] = jnp.zeros(yt_ref.shape, F32)
    lax.fori_loop(0, SCAN_T, step, 0, unroll=SCAN_UNROLL)
    hit = tok_of_lane == SCAN_T - 1
    for prs in groups:
        sums = lane_sums([s_ref[pr] for pr in prs], prs, SCAN_T - 1, SCAN_T - 1)
        for n, pr in enumerate(prs):
            yt_ref[pr] = jnp.where(hit, sums[n * hw:(n + 1) * hw, LANES:2 * LANES], yt_ref[pr])

    ys = []
    for b in streams:
        halves = []
        for hp in range(2):
            z = jnp.concatenate([yt_ref[2 * b + hp], zpad], axis=0).T
            halves.append(z[0:SCAN_T] + pltpu.roll(z[SCAN_T:2 * SCAN_T], hw, 1))
        ys.append(jnp.concatenate(halves, axis=1))
    ys = cat(ys)
    d = ys - _head_sums(ys, ones4) * (1.0 / RWKV_HEAD)
    var = _head_sums(d * d, ones4) * (1.0 / RWKV_HEAD)
    yn = d * lax.rsqrt(var + GN_EPS) * lnw_ref[...] + lnb_ref[...]
    for b in streams:
        yc_ref[b] = ((cut(yn, b) + bonus_v[b]) * _silu(gc_ref[b])).astype(BF16)

    @pl.when(tb == nblk - 1)
    def _():
        conv_ref[...] = cu_ref[...]
        shift_ref[...] = cz_ref[...]
        for pr in range(npair):
            b, hp = divmod(pr, 2)
            wkv_ref[b, 2 * hp] = s_ref[pr, :, 0:hw]
            wkv_ref[b, 2 * hp + 1] = s_ref[pr, :, hw:2 * hw]


def _mix(za, zc, gc, conv0, shift0, wkv0, lw, batch, t, nstream):
    groups = batch // nstream
    nblk = t // SCAN_T
    npair = nstream * RWKV_HEADS // 2
    seq = lambda w: pl.BlockSpec((nstream, SCAN_T, w), lambda g, i: (g, i, 0))
    state3 = lambda a: pl.BlockSpec((nstream,) + a.shape[1:], lambda g, i: (g, 0, 0))
    state4 = pl.BlockSpec((nstream, RWKV_HEADS, RWKV_HEAD, RWKV_HEAD),
                          lambda g, i: (g, 0, 0, 0))
    full = lambda a: pl.BlockSpec(a.shape, lambda g, i: (0,) * a.ndim)
    r_, c_ = np.arange(2 * LANES)[:, None], np.arange(2 * LANES)[None, :]
    wtab = jnp.asarray(((r_ & (SCAN_T - 1)) == np.arange(SCAN_T)[:, None, None])
                       & ((r_ >> HEAD_SHIFT) == (c_ >> HEAD_SHIFT)), dtype=BF16)
    params = (lw["conv_w"], lw["shift_mu"], lw["decay_w0"], lw["w2a2"], lw["iclr_a0"],
              lw["key_kk"], lw["key_ka"], lw["bonus_rk"], lw["lnx_w"], lw["lnx_b"], wtab)
    out_shape = (
        jax.ShapeDtypeStruct((batch, t, CONV_W), BF16),
        jax.ShapeDtypeStruct((batch, t, RWKV_W), BF16),
        jax.ShapeDtypeStruct(conv0.shape, F32),
        jax.ShapeDtypeStruct(shift0.shape, F32),
        jax.ShapeDtypeStruct(wkv0.shape, F32),
    )
    return pl.pallas_call(
        functools.partial(_mix_kernel, nblk=nblk),
        grid=(groups, nblk),
        in_specs=[seq(4 * CONV_W), seq(SHIFT_W), seq(RWKV_W), state3(conv0), state3(shift0),
                  state4] + [full(p) for p in params],
        out_specs=(seq(CONV_W), seq(RWKV_W), state3(conv0), state3(shift0), state4),
        out_shape=out_shape,
        scratch_shapes=[
            pltpu.VMEM((npair, RWKV_HEAD, LANES), F32),
            pltpu.VMEM((nstream, CONV_K - 1, CONV_W), F32),
            pltpu.VMEM((nstream, 1, SHIFT_W), F32),
            pltpu.VMEM((5, npair, SCAN_T, LANES), F32),
            pltpu.VMEM((npair // 2 * RWKV_HEAD, 2 * LANES), BF16),
            pltpu.VMEM((npair, RWKV_HEAD, LANES), F32),
        ],
        compiler_params=pltpu.CompilerParams(dimension_semantics=("parallel", "arbitrary"),
                                             vmem_limit_bytes=VMEM_LIMIT),
        name="mix",
    )(za, zc, gc, conv0, shift0, wkv0, *params)


def _out_kernel(ya_ref, yb_ref, yc_ref, x_ref, wa_ref, wb_ref, wc_ref, fg_ref, o_ref, *, final):
    y = (_dot(ya_ref[...], wa_ref[...]) + _dot(yb_ref[...], wb_ref[...])
         + _dot(yc_ref[...], wc_ref[...]))
    y = x_ref[...] + y
    if final:
        y = _rms(y, fg_ref[...])
    o_ref[...] = y


def _out(ya, yb, yc, x, lw, final_g, final, tm, t=None, skip=0):
    rows = x.shape[0]
    if skip:
        assert skip % tm == 0 and t % tm == 0
        n_skip, n_keep = skip // tm, (t - skip) // tm
        src = lambda i: (i + (i // n_keep + 1) * n_skip, 0)
        out_rows = rows // t * (t - skip)
    else:
        src = lambda i: (i, 0)
        out_rows = rows
    row = lambda w: pl.BlockSpec((tm, w), src)
    full = lambda a: pl.BlockSpec(a.shape, lambda i: (0,) * a.ndim)
    weights = (lw["wo_a"], lw["wo_b"], lw["wo_c"], final_g)
    return pl.pallas_call(
        functools.partial(_out_kernel, final=final),
        grid=(out_rows // tm,),
        in_specs=[row(CONV_W), row(MLA_W), row(RWKV_W), row(D_MODEL)] + [full(w) for w in weights],
        out_specs=pl.BlockSpec((tm, D_MODEL), lambda i: (i, 0)),
        out_shape=jax.ShapeDtypeStruct((out_rows, D_MODEL), F32),
        compiler_params=pltpu.CompilerParams(dimension_semantics=("parallel",),
                                             vmem_limit_bytes=VMEM_LIMIT),
        name="out",
    )(ya, yb, yc, x, *weights)


def _layer_weights(l, norm_g, w_in, conv_w, q_norm_g, w_uq, kv_norm_g, w_ukv, shift_mu,
                   decay_w0, decay_w2, iclr_a0, iclr_a2, key_kk, key_ka, bonus_rk, lnx_w, lnx_b,
                   w_out):
    row = lambda p: p[l].reshape(1, -1).astype(F32)
    wi = w_in[l].astype(BF16)
    o = np.cumsum([0, 4 * CONV_W, Q_LORA, KV_LORA, QK_ROPE, MLA_W, SHIFT_W, RWKV_W]).tolist()
    cols = lambda j: wi[:, o[j]:o[j + 1]]
    uq = w_uq[l].astype(BF16).reshape(Q_LORA, MLA_HEADS, QK_NOPE + QK_ROPE)
    uq_rope = jnp.pad(uq[:, :, QK_NOPE:], ((0, 0), (0, 0), (0, LANES - QK_ROPE)))
    ukv = w_ukv[l].astype(BF16).reshape(KV_LORA, MLA_HEADS, QK_NOPE + V_DIM)
    eye = jnp.eye(MLA_HEADS, dtype=BF16)
    bd = jnp.einsum("chd,hg->hdgc", ukv[:, :, :QK_NOPE], eye).reshape(
        MLA_HEADS * QK_NOPE, MLA_HEADS * KV_LORA)
    wv = jnp.einsum("che,hg->hcge", ukv[:, :, QK_NOPE:], eye).reshape(
        MLA_HEADS, KV_LORA, MLA_W)
    zeros = jnp.zeros((DECAY_LORA, RWKV_W), BF16)
    w2a2 = jnp.concatenate([
        jnp.concatenate([decay_w2[l].astype(BF16), zeros], axis=1),
        jnp.concatenate([zeros, iclr_a2[l].astype(BF16)], axis=1)], axis=0)
    wo = w_out[l].astype(BF16)
    return {
        "norm_g": row(norm_g), "wa": cols(0), "wq": cols(1), "wckv": cols(2),
        "wkr": jnp.pad(cols(3), ((0, 0), (0, LANES - QK_ROPE))),
        "wgb": cols(4), "wzc": cols(5), "wgc": cols(6),
        "q_norm_g": row(q_norm_g),
        "wuq": jnp.concatenate([uq[:, :, :QK_NOPE].reshape(Q_LORA, -1),
                                uq_rope.reshape(Q_LORA, -1)], axis=1),
        "bd": bd, "wv": wv, "kv_norm_g": row(kv_norm_g),
        "conv_w": conv_w[l].astype(F32), "shift_mu": row(shift_mu), "decay_w0": row(decay_w0),
        "w2a2": w2a2, "iclr_a0": row(iclr_a0), "key_kk": row(key_kk), "key_ka": row(key_ka),
        "bonus_rk": row(bonus_rk), "lnx_w": row(lnx_w), "lnx_b": row(lnx_b),
        "wo_a": wo[0:CONV_W], "wo_b": wo[CONV_W:CONV_W + MLA_W], "wo_c": wo[CONV_W + MLA_W:],
    }


def _rope_tables(pos, front):
    half = QK_ROPE // 2
    n = pos.shape[0]
    inv = ROPE_BASE ** (-jnp.arange(half, dtype=F32) * 2.0 / QK_ROPE)
    ang = pos.astype(F32)[:, None] * inv[None, :]
    cos, sin = jnp.cos(ang), jnp.sin(ang)
    z = jnp.zeros((n, LANES - QK_ROPE), F32)
    zh = jnp.zeros((n, half), F32)
    flag = (jnp.arange(n) < front).astype(F32)[:, None]
    pad_col = PAD_LANE - LANES - QK_ROPE
    zflag = jnp.concatenate([z[:, :pad_col], flag, z[:, pad_col + 1:]], axis=1)
    return (jnp.concatenate([cos, cos, zflag], axis=1),
            jnp.concatenate([-sin, zh, z], axis=1),
            jnp.concatenate([zh, sin, z], axis=1))


def _stream(x, batch, t, tm, tabs, tab_blocks, states, attn, lws, final_g, nstream, skip=0):
    outs = []
    depth = len(lws)
    for l, lw in enumerate(lws):
        za, qp, kcat, ckv, kr, gb, zc, gc = _proj(x, lw, tabs, tm, tab_blocks)
        yb = attn(l, lw, qp, kcat, gb)
        conv0, shift0, wkv0 = states[l]
        ya, yc, conv, shift, wkv = _mix(
            za.reshape(batch, t, -1), zc.reshape(batch, t, -1), gc.reshape(batch, t, -1),
            conv0, shift0, wkv0, lw, batch, t, nstream)
        last = l == depth - 1
        x = _out(ya.reshape(batch * t, -1), yb, yc.reshape(batch * t, -1), x, lw, final_g, last,
                 skip if last and skip else tm, t, skip if last else 0)
        outs.append((ckv.reshape(batch, t, -1), kr.reshape(batch, t, -1), conv,
                     shift.reshape(batch, -1), wkv))
    return x, outs


def kernel(x_prompt, x_sample, cache_ckv, cache_krope, state_conv, state_shift, state_wkv,
           meta_tokens, norm_g, w_in, conv_w, q_norm_g, w_uq, kv_norm_g, w_ukv, shift_mu,
           decay_w0, decay_w2, iclr_a0, iclr_a2, key_kk, key_ka, bonus_rk, lnx_w, lnx_b,
           w_out, final_g):
    depth = w_in.shape[0]
    bp, seq = x_prompt.shape[:2]
    bs, ts = x_sample.shape[:2]
    past = cache_ckv.shape[2]
    tp = FRONT + N_META + seq
    assert tp % KTILE == 0 and bp % SCAN_GROUP == 0 and bs % SAMPLE_SCAN_GROUP == 0
    assert seq % CHUNK == 0 and ts == SCAN_T and tp % SCAN_T == 0
    lws = [_layer_weights(l, norm_g, w_in, conv_w, q_norm_g, w_uq, kv_norm_g, w_ukv, shift_mu,
                          decay_w0, decay_w2, iclr_a0, iclr_a2, key_kk, key_ka, bonus_rk,
                          lnx_w, lnx_b, w_out) for l in range(depth)]
    fg = final_g.reshape(1, -1).astype(F32)

    xp = jnp.concatenate([
        jnp.zeros((bp, FRONT, D_MODEL), F32),
        jnp.broadcast_to(meta_tokens.astype(F32)[None], (bp, N_META, D_MODEL)),
        x_prompt], axis=1).reshape(bp * tp, D_MODEL)
    tm_p = tp // PROMPT_ROW_TILES
    tabs_p = _rope_tables(jnp.maximum(jnp.arange(tp, dtype=jnp.int32) - FRONT, 0), FRONT)
    zero_states = [(jnp.zeros((bp, CONV_K - 1, CONV_W), F32), jnp.zeros((bp, 1, SHIFT_W), F32),
                    jnp.zeros((bp, RWKV_HEADS, RWKV_HEAD, RWKV_HEAD), F32))] * depth
    attn_p = lambda l, lw, qp, kcat, gb: _attn_prompt(qp, kcat, gb, lw["wv"], bp, tp)
    yp, outs_p = _stream(xp, bp, tp, tm_p, tabs_p, tp // tm_p, zero_states, attn_p, lws, fg,
                         SCAN_GROUP, skip=FRONT + N_META)

    tm_s = 512
    reps = tm_s // ts
    tabs_s = tuple(jnp.tile(tb_, (reps, 1))
                   for tb_ in _rope_tables(past + jnp.arange(ts, dtype=jnp.int32), 0))
    states_s = [(state_conv[l], state_shift[l].reshape(bs, 1, SHIFT_W), state_wkv[l])
                for l in range(depth)]
    attn_s = lambda l, lw, qp, kcat, gb: _attn_sample(qp, kcat, cache_ckv, cache_krope, l, gb,
                                                      lw["wv"], bs, ts)
    ys, outs_s = _stream(x_sample.reshape(bs * ts, D_MODEL), bs, ts, tm_s, tabs_s, 1,
                         states_s, attn_s, lws, fg, SAMPLE_SCAN_GROUP)

    stack = lambda outs, j: jnp.stack([o[j] for o in outs])
    y_prompt = yp.reshape(bp, seq, D_MODEL)
    ckv_p = stack(outs_p, 0)[:, :, FRONT:]
    kr_p = stack(outs_p, 1)[:, :, FRONT:]
    return (y_prompt, ys.reshape(bs, ts, D_MODEL), ckv_p, kr_p, stack(outs_p, 2),
            stack(outs_p, 3), stack(outs_p, 4),
            stack(outs_s, 0), stack(outs_s, 1), stack(outs_s, 2), stack(outs_s, 3),
            stack(outs_s, 4))
```

```python
import functools

import jax
import jax.numpy as jnp
import numpy as np
from jax import lax
from jax.experimental import pallas as pl
from jax.experimental.pallas import tpu as pltpu

D_MODEL = 1024
CHUNK = 64
N_META = 16
CONV_W = 256
CONV_K = 3
MLA_HEADS = 8
QK_NOPE = 64
QK_ROPE = 32
V_DIM = 64
MLA_W = MLA_HEADS * V_DIM
Q_LORA = 256
KV_LORA = 128
RWKV_HEAD = 64
RWKV_W = 256
RWKV_HEADS = 4
DECAY_LORA = 64
ICLR_LORA = 64
SHIFT_W = 3 * RWKV_W + DECAY_LORA + ICLR_LORA
ROPE_BASE = 10000.0
RMS_EPS = 1e-6
GN_EPS = 64e-5
NEG = -1e30
LOG2_E = 1.4426950408889634
CHUNK_SHIFT = CHUNK.bit_length() - 1
HEAD_SHIFT = RWKV_HEAD.bit_length() - 1
DECAY_SCALE = 0.6065306597126334

LANES = 128
QTILE = 256
KTILE = 256
FRONT = KTILE - N_META
QK_W = 2 * LANES
ONES_LANE = KV_LORA + QK_ROPE
PAD_LANE = ONES_LANE + 1
PROMPT_ROW_TILES = 8
SCAN_GROUP = 4
SAMPLE_SCAN_GROUP = 8
SCAN_T = 64
SCAN_SPLIT = 2
SCAN_UNROLL = 64
VMEM_LIMIT = 48 * 1024 * 1024
ATTN_VMEM_LIMIT = 58 * 1024 * 1024

F32 = jnp.float32
BF16 = jnp.bfloat16


def _dot(a, b):
    return jnp.dot(a, b, preferred_element_type=F32)


def _dot_nt(a, b):
    return lax.dot_general(a, b, (((1,), (1,)), ((), ())), preferred_element_type=F32)


def _rms(x, g):
    return x * lax.rsqrt(jnp.mean(x * x, axis=-1, keepdims=True) + RMS_EPS) * g


def _silu(x):
    return x * jax.nn.sigmoid(x)


def _rope_block(x, cos, sin_lo, sin_hi):
    return (x * cos + pltpu.roll(x, LANES - QK_ROPE // 2, 1) * sin_lo
            + pltpu.roll(x, QK_ROPE // 2, 1) * sin_hi)


def _proj_kernel(x_ref, g_ref, wa_ref, wq_ref, wckv_ref, wkr_ref, wgb_ref, wzc_ref, wgc_ref,
                 qg_ref, wuq_ref, bd_ref, kvg_ref, cos_ref, slo_ref, shi_ref,
                 za_ref, qp_ref, kcat_ref, ckv_ref, kr_ref, gb_ref, zc_ref, gc_ref):
    scale = (QK_NOPE + QK_ROPE) ** -0.5 * LOG2_E
    hb = _rms(x_ref[...], g_ref[...]).astype(BF16)
    za_ref[...] = _dot(hb, wa_ref[...])
    gb_ref[...] = _dot(hb, wgb_ref[...])
    zc_ref[...] = _dot(hb, wzc_ref[...])
    gc_ref[...] = _dot(hb, wgc_ref[...])

    cos, slo, shi = cos_ref[...], slo_ref[...], shi_ref[...]
    cqn = _rms(_dot(hb, wq_ref[...]), qg_ref[...]).astype(BF16)
    qe = _dot(cqn, wuq_ref[...])
    nope_w = MLA_HEADS * QK_NOPE
    qabs = _dot(qe[:, :nope_w].astype(BF16), bd_ref[...])
    lane = lax.broadcasted_iota(jnp.int32, (1, LANES), 1)
    ones_lane, pad_lane = lane == ONES_LANE - LANES, lane == PAD_LANE - LANES
    for h in range(MLA_HEADS):
        lo, hi = h * LANES, (h + 1) * LANES
        rot = _rope_block(qe[:, nope_w + lo:nope_w + hi], cos, slo, shi)
        qp_ref[h, :, 0:LANES] = (qabs[:, lo:hi] * scale).astype(BF16)
        qp_ref[h, :, LANES:QK_W] = jnp.where(pad_lane, NEG, rot * scale).astype(BF16)

    ckvn = _rms(_dot(hb, wckv_ref[...]), kvg_ref[...])
    ckv_ref[...] = ckvn
    krr = _rope_block(_dot(hb, wkr_ref[...]), cos, slo, shi)
    kr_ref[...] = krr[:, :QK_ROPE]
    kcat_ref[:, 0:LANES] = ckvn.astype(BF16)
    krr = jnp.where(ones_lane, 1.0, jnp.where(pad_lane, cos, krr))
    kcat_ref[:, LANES:QK_W] = krr.astype(BF16)


def _proj(x, lw, tabs, tm, tab_blocks):
    rows = x.shape[0]
    n = rows // tm
    row = lambda w: pl.BlockSpec((tm, w), lambda i: (i, 0))
    full = lambda a: pl.BlockSpec(a.shape, lambda i: (0,) * a.ndim)
    tab = pl.BlockSpec((tm, LANES), lambda i: (i % tab_blocks, 0))
    weights = (lw["norm_g"], lw["wa"], lw["wq"], lw["wckv"], lw["wkr"], lw["wgb"], lw["wzc"],
               lw["wgc"], lw["q_norm_g"], lw["wuq"], lw["bd"], lw["kv_norm_g"])
    out_shape = (
        jax.ShapeDtypeStruct((rows, 4 * CONV_W), F32),
        jax.ShapeDtypeStruct((MLA_HEADS, rows, QK_W), BF16),
        jax.ShapeDtypeStruct((rows, QK_W), BF16),
        jax.ShapeDtypeStruct((rows, KV_LORA), F32),
        jax.ShapeDtypeStruct((rows, QK_ROPE), F32),
        jax.ShapeDtypeStruct((rows, MLA_W), F32),
        jax.ShapeDtypeStruct((rows, SHIFT_W), F32),
        jax.ShapeDtypeStruct((rows, RWKV_W), F32),
    )
    out_specs = (row(4 * CONV_W), pl.BlockSpec((MLA_HEADS, tm, QK_W), lambda i: (0, i, 0)),
                 row(QK_W), row(KV_LORA), row(QK_ROPE), row(MLA_W), row(SHIFT_W), row(RWKV_W))
    return pl.pallas_call(
        _proj_kernel,
        grid=(n,),
        in_specs=[row(D_MODEL)] + [full(w) for w in weights] + [tab, tab, tab],
        out_specs=out_specs,
        out_shape=out_shape,
        compiler_params=pltpu.CompilerParams(dimension_semantics=("parallel",),
                                             vmem_limit_bytes=VMEM_LIMIT),
        name="proj",
    )(x, *weights, *tabs)


def _heads_out(o_lat, wv_ref, rows):
    out = None
    for h in range(MLA_HEADS):
        part = _dot(o_lat[h * rows:(h + 1) * rows].astype(BF16), wv_ref[h])
        out = part if out is None else out + part
    return out


def _attn_prompt_kernel(q_ref, k_ref, gb_ref, wv_ref, yb_ref, s_ref, mx_ref, acc_ref):
    i = pl.program_id(1)
    rows = MLA_HEADS * QTILE

    all_front = (i + 1) * QTILE <= FRONT

    @pl.when(all_front)
    def _():
        yb_ref[...] = jnp.zeros(yb_ref.shape, BF16)

    @pl.when(jnp.logical_not(all_front))
    def _():
        q = q_ref[...].reshape(rows, QK_W)
        lane = lax.broadcasted_iota(jnp.int32, (rows, LANES), 1)
        qrow = (lax.broadcasted_iota(jnp.int32, (rows, LANES), 0) & (QTILE - 1)) + i * QTILE
        last = i * (QTILE // KTILE)

        def run_tiles(start, count, fn):
            def quad(jj, carry):
                fn(start + 4 * jj, 4 * KTILE)
                return carry

            lax.fori_loop(0, count >> 2, quad, 0)
            done = start + (count & ~3)

            @pl.when((count & 2) != 0)
            def _():
                fn(done, 2 * KTILE)

            @pl.when((count & 1) != 0)
            def _():
                fn(done + (count & 2), KTILE)

        def keys(j, width):
            return k_ref[0, pl.ds(pl.multiple_of(j * KTILE, KTILE), width), :]

        def s_at(j, width):
            return s_ref.at[:, pl.ds(pl.multiple_of(j * KTILE, KTILE), width)]

        def fold(x, op):
            out = x[:, :LANES]
            for c in range(1, x.shape[1] // LANES):
                out = op(out, x[:, c * LANES:(c + 1) * LANES])
            return out

        last_visible = qrow | (CHUNK - 1)
        mx_ref[...] = jnp.full((rows, LANES), NEG, F32)

        def scores(j, width):
            s = _dot_nt(q, keys(j, width))
            blocks = []
            for c in range(width // LANES):
                key_row = lane + (j * KTILE + c * LANES)
                blocks.append(jnp.where(key_row <= last_visible,
                                        s[:, c * LANES:(c + 1) * LANES], NEG))
            s = jnp.concatenate(blocks, axis=1)
            s_at(j, width)[...] = s
            mx_ref[...] = jnp.maximum(mx_ref[...], fold(s, jnp.maximum))

        run_tiles(0, last + 1, scores)

        m = jnp.broadcast_to(jnp.max(mx_ref[...], axis=-1, keepdims=True), (rows, LANES))
        mx_ref[...] = m
        acc_ref[...] = jnp.zeros((rows, QK_W), F32)

        def weigh(j, width):
            s = s_at(j, width)[...]
            m_ = mx_ref[...]
            pb = jnp.concatenate([jnp.exp2(s[:, c * LANES:(c + 1) * LANES] - m_).astype(BF16)
                                  for c in range(width // LANES)], axis=1)
            acc_ref[...] += _dot(pb, keys(j, width))

        run_tiles(0, last + 1, weigh)

        acc = acc_ref[...]
        o_lat = acc[:, :KV_LORA] / acc[:, ONES_LANE:ONES_LANE + 1]
        yb_ref[...] = (_heads_out(o_lat, wv_ref, QTILE) * _silu(gb_ref[...])).astype(BF16)


def _attn_prompt(qp, kcat, gb, wv, batch, tp):
    nq = tp // QTILE
    rows = MLA_HEADS * QTILE
    return pl.pallas_call(
        _attn_prompt_kernel,
        grid=(batch, nq),
        in_specs=[
            pl.BlockSpec((MLA_HEADS, QTILE, QK_W), lambda b, i: (0, b * nq + i, 0)),
            pl.BlockSpec((1, tp, QK_W), lambda b, i: (b, 0, 0)),
            pl.BlockSpec((QTILE, MLA_W), lambda b, i: (b * nq + i, 0)),
            pl.BlockSpec(wv.shape, lambda b, i: (0, 0, 0)),
        ],
        out_specs=pl.BlockSpec((QTILE, MLA_W), lambda b, i: (b * nq + i, 0)),
        out_shape=jax.ShapeDtypeStruct((batch * tp, MLA_W), BF16),
        scratch_shapes=[pltpu.VMEM((rows, tp), F32),
                        pltpu.VMEM((rows, LANES), F32),
                        pltpu.VMEM((rows, QK_W), F32)],
        compiler_params=pltpu.CompilerParams(dimension_semantics=("parallel", "arbitrary"),
                                             vmem_limit_bytes=ATTN_VMEM_LIMIT),
        name="attn_prompt",
    )(qp, kcat.reshape(batch, tp, QK_W), gb, wv)


def _attn_sample_kernel(q_ref, kn_ref, ck_ref, kr_ref, gb_ref, wv_ref, yb_ref, kp_ref):
    ts = q_ref.shape[1]
    q = q_ref[...].reshape(MLA_HEADS * ts, QK_W)
    kn = kn_ref[...]
    ck = ck_ref[0, 0].astype(BF16)
    kp_ref[:, 0:KV_LORA] = ck
    kp_ref[:, KV_LORA:QK_W] = jnp.zeros((kp_ref.shape[0], QK_W - KV_LORA), BF16)
    kp_ref[:, KV_LORA:KV_LORA + QK_ROPE] = kr_ref[0, 0].astype(BF16)
    s_past = _dot_nt(q, kp_ref[...])
    s_new = _dot_nt(q, kn)
    m = jnp.maximum(jnp.max(s_past, axis=-1, keepdims=True),
                    jnp.max(s_new, axis=-1, keepdims=True))
    p_past = jnp.exp2(s_past - m)
    p_new = jnp.exp2(s_new - m)
    l = jnp.sum(p_past, axis=-1, keepdims=True) + jnp.sum(p_new, axis=-1, keepdims=True)
    o_lat = (_dot(p_past.astype(BF16), ck) + _dot(p_new.astype(BF16), kn[:, :KV_LORA])) / l
    yb_ref[...] = (_heads_out(o_lat, wv_ref, ts) * _silu(gb_ref[...])).astype(BF16)


def _attn_sample(qp, kcat, cache_ckv, cache_krope, layer, gb, wv, batch, ts):
    past = cache_ckv.shape[2]
    return pl.pallas_call(
        _attn_sample_kernel,
        grid=(batch,),
        in_specs=[
            pl.BlockSpec((MLA_HEADS, ts, QK_W), lambda b: (0, b, 0)),
            pl.BlockSpec((ts, QK_W), lambda b: (b, 0)),
            pl.BlockSpec((1, 1, past, KV_LORA), lambda b: (layer, b, 0, 0)),
            pl.BlockSpec((1, 1, past, QK_ROPE), lambda b: (layer, b, 0, 0)),
            pl.BlockSpec((ts, MLA_W), lambda b: (b, 0)),
            pl.BlockSpec(wv.shape, lambda b: (0, 0, 0)),
        ],
        out_specs=pl.BlockSpec((ts, MLA_W), lambda b: (b, 0)),
        out_shape=jax.ShapeDtypeStruct((batch * ts, MLA_W), BF16),
        scratch_shapes=[pltpu.VMEM((past, QK_W), BF16)],
        compiler_params=pltpu.CompilerParams(dimension_semantics=("parallel",),
                                             vmem_limit_bytes=VMEM_LIMIT),
        name="attn_sample",
    )(qp, kcat, cache_ckv, cache_krope, gb, wv)


def _head_sums(x, ones4):
    hi = x.astype(BF16)
    lo = (x - hi.astype(F32)).astype(BF16)
    both = _dot(jnp.concatenate([hi, lo], axis=0), ones4)
    return both[0:x.shape[0]] + both[x.shape[0]:]


def _mix_kernel(za_ref, zc_ref, gc_ref, conv0_ref, shift0_ref, wkv0_ref,
                cw_ref, mu_ref, w0_ref, w2a2_ref, a0_ref, kkw_ref, kaw_ref, bonus_ref,
                lnw_ref, lnb_ref, wtab_ref,
                ya_ref, yc_ref, conv_ref, shift_ref, wkv_ref,
                s_ref, cu_ref, cz_ref, vec_ref, vtl_ref, yt_ref, *, nblk):
    tb = pl.program_id(1)
    tblk = SCAN_T
    nstream = za_ref.shape[0]
    npair = nstream * RWKV_HEADS // 2
    hw = RWKV_HEAD

    @pl.when(tb == 0)
    def _():
        for pr in range(npair):
            b, hp = divmod(pr, 2)
            s_ref[pr, :, 0:hw] = wkv0_ref[b, 2 * hp]
            s_ref[pr, :, hw:2 * hw] = wkv0_ref[b, 2 * hp + 1]
        cu_ref[...] = conv0_ref[...]
        cz_ref[...] = shift0_ref[...]

    trow = lax.broadcasted_iota(jnp.int32, (tblk, 1), 0)
    lane128 = lax.broadcasted_iota(jnp.int32, (1, LANES), 1)
    ri = lax.broadcasted_iota(jnp.int32, (2 * LANES, 2 * LANES), 0)
    ci = lax.broadcasted_iota(jnp.int32, (2 * LANES, 2 * LANES), 1)
    ones4 = jnp.where((ri >> HEAD_SHIFT) == (ci >> HEAD_SHIFT), 1.0, 0.0).astype(BF16)
    cw = cw_ref[...]
    zpad = jnp.zeros((LANES - hw, LANES), F32)
    streams = range(nstream)
    cat = lambda parts: jnp.concatenate(parts, axis=0)
    cut = lambda x, b: x[b * tblk:(b + 1) * tblk]
    rs, ks, vs, las = [], [], [], []
    for b in streams:
        za = za_ref[b]
        xin, bg = za[:, 0:CONV_W], za[:, CONV_W:2 * CONV_W]
        cg, ga = za[:, 2 * CONV_W:3 * CONV_W], za[:, 3 * CONV_W:4 * CONV_W]
        u = cg * xin
        cu = cu_ref[b]
        u1 = jnp.where(trow == 0, cu[1:2], pltpu.roll(u, 1, 0))
        u2 = jnp.where(trow == 0, cu[0:1], jnp.where(trow == 1, cu[1:2], pltpu.roll(u, 2, 0)))
        conv = cw[0:1] * u2 + cw[1:2] * u1 + cw[2:3] * u
        ya_ref[b] = (bg * conv * _silu(ga)).astype(BF16)
        cu_ref[b] = u[tblk - (CONV_K - 1):tblk]

        zc = zc_ref[b]
        prev = jnp.where(trow == 0, cz_ref[b], pltpu.roll(zc, 1, 0))
        cz_ref[b] = zc[tblk - 1:tblk]
        zs = zc + (prev - zc) * mu_ref[...]
        r, k, v = zs[:, 0:RWKV_W], zs[:, RWKV_W:2 * RWKV_W], zs[:, 2 * RWKV_W:3 * RWKV_W]
        la = zs[:, 3 * RWKV_W:SHIFT_W]
        las.append(jnp.where(lane128 < DECAY_LORA, jnp.tanh(la), la).astype(BF16))
        rs.append(r), ks.append(k), vs.append(v)

    lora = _dot(cat(las), w2a2_ref[...])
    decays, avs, kks, kps = [], [], [], []
    for b in streams:
        dec_in = w0_ref[...] + cut(lora, b)[:, 0:RWKV_W]
        decays.append(jnp.exp(-DECAY_SCALE * jax.nn.sigmoid(dec_in)))
        a = jax.nn.sigmoid(a0_ref[...] + cut(lora, b)[:, RWKV_W:2 * RWKV_W])
        avs.append(a)
        kks.append(ks[b] * kkw_ref[...])
        kps.append(ks[b] * (1.0 + (a - 1.0) * kaw_ref[...]))
    kk_sq = _head_sums(cat([kk * kk for kk in kks]), ones4)
    rk_sum = _head_sums(cat([rs[b] * kps[b] * bonus_ref[...] for b in streams]), ones4)
    bonus_v = []
    for b in streams:
        r, v, a, kp, decay = rs[b], vs[b], avs[b], kps[b], decays[b]
        kk = kks[b] / jnp.maximum(jnp.sqrt(cut(kk_sq, b)), 1e-12)
        bonus_v.append(cut(rk_sum, b) * v)
        for n, vec in enumerate((-kk, decay, kk * a, kp, r)):
            vec_ref[n, 2 * b] = vec[:, 0:LANES]
            vec_ref[n, 2 * b + 1] = vec[:, LANES:2 * LANES]
        for hp in range(2):
            z = jnp.concatenate([v[:, hp * LANES:(hp + 1) * LANES], zpad], axis=0).T
            vt = z[0:hw] + pltpu.roll(z[hw:2 * hw], SCAN_T, 1)
            side, n = divmod(2 * b + hp, npair // 2)
            vtl_ref[n * hw:(n + 1) * hw, side * LANES:(side + 1) * LANES] = vt.astype(BF16)

    tok_of_lane = lane128 & (SCAN_T - 1)
    gsz = npair // SCAN_SPLIT
    groups = [list(range(g * gsz, (g + 1) * gsz)) for g in range(SCAN_SPLIT)]

    def lane_sums(s_list, prs, t_a, t_b):
        lhs = jnp.concatenate(
            [jnp.concatenate([(s * vec_ref[0, pr, pl.ds(t_a, 1), :]).astype(BF16),
                              (s * vec_ref[4, pr, pl.ds(t_b, 1), :]).astype(BF16)], axis=1)
             for s, pr in zip(s_list, prs)], axis=0)
        return _dot(lhs, ones4)

    def step(t, carry):
        rowv = lambda n, pr: vec_ref[n, pr, pl.ds(t, 1), :]
        vb = _dot(vtl_ref[...], wtab_ref[t])
        hit = tok_of_lane == t - 1
        for prs in groups:
            s_old = [s_ref[pr] for pr in prs]
            sums = lane_sums(s_old, prs, t, jnp.maximum(t - 1, 0))
            for n, pr in enumerate(prs):
                rows = slice(n * hw, (n + 1) * hw)
                side, m = divmod(pr, npair // 2)
                vbp = vb[m * hw:(m + 1) * hw, side * LANES:(side + 1) * LANES]
                s_ref[pr] = (s_old[n] * rowv(1, pr) + sums[rows, 0:LANES] * rowv(2, pr)
                             + vbp * rowv(3, pr))
                yt_ref[pr] = jnp.where(hit, sums[rows, LANES:2 * LANES], yt_ref[pr])
        return carry

    yt_ref[...] = jnp.zeros(yt_ref.shape, F32)
    lax.fori_loop(0, SCAN_T, step, 0, unroll=SCAN_UNROLL)
    hit = tok_of_lane == SCAN_T - 1
    for prs in groups:
        sums = lane_sums([s_ref[pr] for pr in prs], prs, SCAN_T - 1, SCAN_T - 1)
        for n, pr in enumerate(prs):
            yt_ref[pr] = jnp.where(hit, sums[n * hw:(n + 1) * hw, LANES:2 * LANES], yt_ref[pr])

    ys = []
    for b in streams:
        halves = []
        for hp in range(2):
            z = jnp.concatenate([yt_ref[2 * b + hp], zpad], axis=0).T
            halves.append(z[0:SCAN_T] + pltpu.roll(z[SCAN_T:2 * SCAN_T], hw, 1))
        ys.append(jnp.concatenate(halves, axis=1))
    ys = cat(ys)
    d = ys - _head_sums(ys, ones4) * (1.0 / RWKV_HEAD)
    var = _head_sums(d * d, ones4) * (1.0 / RWKV_HEAD)
    yn = d * lax.rsqrt(var + GN_EPS) * lnw_ref[...] + lnb_ref[...]
    for b in streams:
        yc_ref[b] = ((cut(yn, b) + bonus_v[b]) * _silu(gc_ref[b])).astype(BF16)

    @pl.when(tb == nblk - 1)
    def _():
        conv_ref[...] = cu_ref[...]
        shift_ref[...] = cz_ref[...]
        for pr in range(npair):
            b, hp = divmod(pr, 2)
            wkv_ref[b, 2 * hp] = s_ref[pr, :, 0:hw]
            wkv_ref[b, 2 * hp + 1] = s_ref[pr, :, hw:2 * hw]


def _mix(za, zc, gc, conv0, shift0, wkv0, lw, batch, t, nstream):
    groups = batch // nstream
    nblk = t // SCAN_T
    npair = nstream * RWKV_HEADS // 2
    seq = lambda w: pl.BlockSpec((nstream, SCAN_T, w), lambda g, i: (g, i, 0))
    state3 = lambda a: pl.BlockSpec((nstream,) + a.shape[1:], lambda g, i: (g, 0, 0))
    state4 = pl.BlockSpec((nstream, RWKV_HEADS, RWKV_HEAD, RWKV_HEAD),
                          lambda g, i: (g, 0, 0, 0))
    full = lambda a: pl.BlockSpec(a.shape, lambda g, i: (0,) * a.ndim)
    r_, c_ = np.arange(2 * LANES)[:, None], np.arange(2 * LANES)[None, :]
    wtab = jnp.asarray(((r_ & (SCAN_T - 1)) == np.arange(SCAN_T)[:, None, None])
                       & ((r_ >> HEAD_SHIFT) == (c_ >> HEAD_SHIFT)), dtype=BF16)
    params = (lw["conv_w"], lw["shift_mu"], lw["decay_w0"], lw["w2a2"], lw["iclr_a0"],
              lw["key_kk"], lw["key_ka"], lw["bonus_rk"], lw["lnx_w"], lw["lnx_b"], wtab)
    out_shape = (
        jax.ShapeDtypeStruct((batch, t, CONV_W), BF16),
        jax.ShapeDtypeStruct((batch, t, RWKV_W), BF16),
        jax.ShapeDtypeStruct(conv0.shape, F32),
        jax.ShapeDtypeStruct(shift0.shape, F32),
        jax.ShapeDtypeStruct(wkv0.shape, F32),
    )
    return pl.pallas_call(
        functools.partial(_mix_kernel, nblk=nblk),
        grid=(groups, nblk),
        in_specs=[seq(4 * CONV_W), seq(SHIFT_W), seq(RWKV_W), state3(conv0), state3(shift0),
                  state4] + [full(p) for p in params],
        out_specs=(seq(CONV_W), seq(RWKV_W), state3(conv0), state3(shift0), state4),
        out_shape=out_shape,
        scratch_shapes=[
            pltpu.VMEM((npair, RWKV_HEAD, LANES), F32),
            pltpu.VMEM((nstream, CONV_K - 1, CONV_W), F32),
            pltpu.VMEM((nstream, 1, SHIFT_W), F32),
            pltpu.VMEM((5, npair, SCAN_T, LANES), F32),
            pltpu.VMEM((npair // 2 * RWKV_HEAD, 2 * LANES), BF16),
            pltpu.VMEM((npair, RWKV_HEAD, LANES), F32),
        ],
        compiler_params=pltpu.CompilerParams(dimension_semantics=("parallel", "arbitrary"),
                                             vmem_limit_bytes=VMEM_LIMIT),
        name="mix",
    )(za, zc, gc, conv0, shift0, wkv0, *params)


def _out_kernel(ya_ref, yb_ref, yc_ref, x_ref, wa_ref, wb_ref, wc_ref, fg_ref, o_ref, *, final):
    y = (_dot(ya_ref[...], wa_ref[...]) + _dot(yb_ref[...], wb_ref[...])
         + _dot(yc_ref[...], wc_ref[...]))
    y = x_ref[...] + y
    if final:
        y = _rms(y, fg_ref[...])
    o_ref[...] = y


def _out(ya, yb, yc, x, lw, final_g, final, tm, t=None, skip=0):
    rows = x.shape[0]
    if skip:
        assert skip % tm == 0 and t % tm == 0
        n_skip, n_keep = skip // tm, (t - skip) // tm
        src = lambda i: (i + (i // n_keep + 1) * n_skip, 0)
        out_rows = rows // t * (t - skip)
    else:
        src = lambda i: (i, 0)
        out_rows = rows
    row = lambda w: pl.BlockSpec((tm, w), src)
    full = lambda a: pl.BlockSpec(a.shape, lambda i: (0,) * a.ndim)
    weights = (lw["wo_a"], lw["wo_b"], lw["wo_c"], final_g)
    return pl.pallas_call(
        functools.partial(_out_kernel, final=final),
        grid=(out_rows // tm,),
        in_specs=[row(CONV_W), row(MLA_W), row(RWKV_W), row(D_MODEL)] + [full(w) for w in weights],
        out_specs=pl.BlockSpec((tm, D_MODEL), lambda i: (i, 0)),
        out_shape=jax.ShapeDtypeStruct((out_rows, D_MODEL), F32),
        compiler_params=pltpu.CompilerParams(dimension_semantics=("parallel",),
                                             vmem_limit_bytes=VMEM_LIMIT),
        name="out",
    )(ya, yb, yc, x, *weights)


def _layer_weights(l, norm_g, w_in, conv_w, q_norm_g, w_uq, kv_norm_g, w_ukv, shift_mu,
                   decay_w0, decay_w2, iclr_a0, iclr_a2, key_kk, key_ka, bonus_rk, lnx_w, lnx_b,
                   w_out):
    row = lambda p: p[l].reshape(1, -1).astype(F32)
    wi = w_in[l].astype(BF16)
    o = np.cumsum([0, 4 * CONV_W, Q_LORA, KV_LORA, QK_ROPE, MLA_W, SHIFT_W, RWKV_W]).tolist()
    cols = lambda j: wi[:, o[j]:o[j + 1]]
    uq = w_uq[l].astype(BF16).reshape(Q_LORA, MLA_HEADS, QK_NOPE + QK_ROPE)
    uq_rope = jnp.pad(uq[:, :, QK_NOPE:], ((0, 0), (0, 0), (0, LANES - QK_ROPE)))
    ukv = w_ukv[l].astype(BF16).reshape(KV_LORA, MLA_HEADS, QK_NOPE + V_DIM)
    eye = jnp.eye(MLA_HEADS, dtype=BF16)
    bd = jnp.einsum("chd,hg->hdgc", ukv[:, :, :QK_NOPE], eye).reshape(
        MLA_HEADS * QK_NOPE, MLA_HEADS * KV_LORA)
    wv = jnp.einsum("che,hg->hcge", ukv[:, :, QK_NOPE:], eye).reshape(
        MLA_HEADS, KV_LORA, MLA_W)
    zeros = jnp.zeros((DECAY_LORA, RWKV_W), BF16)
    w2a2 = jnp.concatenate([
        jnp.concatenate([decay_w2[l].astype(BF16), zeros], axis=1),
        jnp.concatenate([zeros, iclr_a2[l].astype(BF16)], axis=1)], axis=0)
    wo = w_out[l].astype(BF16)
    return {
        "norm_g": row(norm_g), "wa": cols(0), "wq": cols(1), "wckv": cols(2),
        "wkr": jnp.pad(cols(3), ((0, 0), (0, LANES - QK_ROPE))),
        "wgb": cols(4), "wzc": cols(5), "wgc": cols(6),
        "q_norm_g": row(q_norm_g),
        "wuq": jnp.concatenate([uq[:, :, :QK_NOPE].reshape(Q_LORA, -1),
                                uq_rope.reshape(Q_LORA, -1)], axis=1),
        "bd": bd, "wv": wv, "kv_norm_g": row(kv_norm_g),
        "conv_w": conv_w[l].astype(F32), "shift_mu": row(shift_mu), "decay_w0": row(decay_w0),
        "w2a2": w2a2, "iclr_a0": row(iclr_a0), "key_kk": row(key_kk), "key_ka": row(key_ka),
        "bonus_rk": row(bonus_rk), "lnx_w": row(lnx_w), "lnx_b": row(lnx_b),
        "wo_a": wo[0:CONV_W], "wo_b": wo[CONV_W:CONV_W + MLA_W], "wo_c": wo[CONV_W + MLA_W:],
    }


def _rope_tables(pos, front):
    half = QK_ROPE // 2
    n = pos.shape[0]
    inv = ROPE_BASE ** (-jnp.arange(half, dtype=F32) * 2.0 / QK_ROPE)
    ang = pos.astype(F32)[:, None] * inv[None, :]
    cos, sin = jnp.cos(ang), jnp.sin(ang)
    z = jnp.zeros((n, LANES - QK_ROPE), F32)
    zh = jnp.zeros((n, half), F32)
    flag = (jnp.arange(n) < front).astype(F32)[:, None]
    pad_col = PAD_LANE - LANES - QK_ROPE
    zflag = jnp.concatenate([z[:, :pad_col], flag, z[:, pad_col + 1:]], axis=1)
    return (jnp.concatenate([cos, cos, zflag], axis=1),
            jnp.concatenate([-sin, zh, z], axis=1),
            jnp.concatenate([zh, sin, z], axis=1))


def _stream(x, batch, t, tm, tabs, tab_blocks, states, attn, lws, final_g, nstream, skip=0):
    outs = []
    depth = len(lws)
    for l, lw in enumerate(lws):
        za, qp, kcat, ckv, kr, gb, zc, gc = _proj(x, lw, tabs, tm, tab_blocks)
        yb = attn(l, lw, qp, kcat, gb)
        conv0, shift0, wkv0 = states[l]
        ya, yc, conv, shift, wkv = _mix(
            za.reshape(batch, t, -1), zc.reshape(batch, t, -1), gc.reshape(batch, t, -1),
            conv0, shift0, wkv0, lw, batch, t, nstream)
        last = l == depth - 1
        x = _out(ya.reshape(batch * t, -1), yb, yc.reshape(batch * t, -1), x, lw, final_g, last,
                 skip if last and skip else tm, t, skip if last else 0)
        outs.append((ckv.reshape(batch, t, -1), kr.reshape(batch, t, -1), conv,
                     shift.reshape(batch, -1), wkv))
    return x, outs


def kernel(x_prompt, x_sample, cache_ckv, cache_krope, state_conv, state_shift, state_wkv,
           meta_tokens, norm_g, w_in, conv_w, q_norm_g, w_uq, kv_norm_g, w_ukv, shift_mu,
           decay_w0, decay_w2, iclr_a0, iclr_a2, key_kk, key_ka, bonus_rk, lnx_w, lnx_b,
           w_out, final_g):
    depth = w_in.shape[0]
    bp, seq = x_prompt.shape[:2]
    bs, ts = x_sample.shape[:2]
    past = cache_ckv.shape[2]
    tp = FRONT + N_META + seq
    assert tp % KTILE == 0 and bp % SCAN_GROUP == 0 and bs % SAMPLE_SCAN_GROUP == 0
    assert seq % CHUNK == 0 and ts == SCAN_T and tp % SCAN_T == 0
    lws = [_layer_weights(l, norm_g, w_in, conv_w, q_norm_g, w_uq, kv_norm_g, w_ukv, shift_mu,
                          decay_w0, decay_w2, iclr_a0, iclr_a2, key_kk, key_ka, bonus_rk,
                          lnx_w, lnx_b, w_out) for l in range(depth)]
    fg = final_g.reshape(1, -1).astype(F32)

    xp = jnp.concatenate([
        jnp.zeros((bp, FRONT, D_MODEL), F32),
        jnp.broadcast_to(meta_tokens.astype(F32)[None], (bp, N_META, D_MODEL)),
        x_prompt], axis=1).reshape(bp * tp, D_MODEL)
    tm_p = tp // PROMPT_ROW_TILES
    tabs_p = _rope_tables(jnp.maximum(jnp.arange(tp, dtype=jnp.int32) - FRONT, 0), FRONT)
    zero_states = [(jnp.zeros((bp, CONV_K - 1, CONV_W), F32), jnp.zeros((bp, 1, SHIFT_W), F32),
                    jnp.zeros((bp, RWKV_HEADS, RWKV_HEAD, RWKV_HEAD), F32))] * depth
    attn_p = lambda l, lw, qp, kcat, gb: _attn_prompt(qp, kcat, gb, lw["wv"], bp, tp)
    yp, outs_p = _stream(xp, bp, tp, tm_p, tabs_p, tp // tm_p, zero_states, attn_p, lws, fg,
                         SCAN_GROUP, skip=FRONT + N_META)

    tm_s = 512
    reps = tm_s // ts
    tabs_s = tuple(jnp.tile(tb_, (reps, 1))
                   for tb_ in _rope_tables(past + jnp.arange(ts, dtype=jnp.int32), 0))
    states_s = [(state_conv[l], state_shift[l].reshape(bs, 1, SHIFT_W), state_wkv[l])
                for l in range(depth)]
    attn_s = lambda l, lw, qp, kcat, gb: _attn_sample(qp, kcat, cache_ckv, cache_krope, l, gb,
                                                      lw["wv"], bs, ts)
    ys, outs_s = _stream(x_sample.reshape(bs * ts, D_MODEL), bs, ts, tm_s, tabs_s, 1,
                         states_s, attn_s, lws, fg, SAMPLE_SCAN_GROUP)

    stack = lambda outs, j: jnp.stack([o[j] for o in outs])
    y_prompt = yp.reshape(bp, seq, D_MODEL)
    ckv_p = stack(outs_p, 0)[:, :, FRONT:]
    kr_p = stack(outs_p, 1)[:, :, FRONT:]
    return (y_prompt, ys.reshape(bs, ts, D_MODEL), ckv_p, kr_p, stack(outs_p, 2),
            stack(outs_p, 3), stack(outs_p, 4),
            stack(outs_s, 0), stack(outs_s, 1), stack(outs_s, 2), stack(outs_s, 3),
            stack(outs_s, 4))
```
